```python
import math
import jax, jax.numpy as jnp
from jax import lax
import numpy as np

D_MODEL = 4096
BATCH = 1
SEQ = 8192
DEPTH = 1
DEC_BATCH = 128
DEC_SEQ = 1
PAST_LEN = 2048
PAGE_SIZE = 128

HEAD_DIM = 128
NSA_HEADS = 16
NSA_KV = 4
SB_HEADS = 12
SB_KV = 4
MEM_HEADS = 4
N_MEM = 256
CMP_BLOCK = 32
CMP_HIDDEN = 256
SEL_BLOCK = 64
N_SEL = 16
WINDOW = 512
Q_BLOCK = 128
ROPE_THETA = 500000.0
ROPE_DIMS = HEAD_DIM // 4
D_FF = 4 * D_MODEL
N_BRANCH = 3
EPS = 1e-6
NEG = -1e30
FORCE_BONUS = 1e4

NSA_QW = NSA_HEADS * HEAD_DIM
NSA_KVW = NSA_KV * HEAD_DIM
SB_QW = SB_HEADS * HEAD_DIM
SB_KVW = SB_KV * HEAD_DIM
MEM_W = MEM_HEADS * HEAD_DIM
COL_SPLITS = (NSA_QW, NSA_KVW, NSA_KVW, NSA_KVW, NSA_KVW, NSA_KVW, NSA_KVW, NSA_HEADS * 3,
              SB_QW, SB_KVW, SB_KVW, MEM_W, N_BRANCH * D_MODEL)
IN_COLS = sum(COL_SPLITS)

kernel_name = "nsa_stickbreaking_memory_hybrid_step"


def _rmsnorm(x, w):
    xf = x.astype(jnp.float32)
    y = xf * lax.rsqrt(jnp.mean(xf * xf, axis=-1, keepdims=True) + EPS)
    return (y * w.astype(jnp.float32)).astype(x.dtype)


def _rope(x, pos):
    half = ROPE_DIMS // 2
    inv = jnp.power(ROPE_THETA, -jnp.arange(half, dtype=jnp.float32) * 2.0 / ROPE_DIMS)
    ang = pos.astype(jnp.float32)[:, None] * inv[None, :]
    cos = jnp.cos(ang)[None, :, None, :]
    sin = jnp.sin(ang)[None, :, None, :]
    xr = x[..., :ROPE_DIMS].astype(jnp.float32)
    x1, x2 = xr[..., :half], xr[..., half:]
    rot = jnp.concatenate([x1 * cos - x2 * sin, x2 * cos + x1 * sin], axis=-1)
    return jnp.concatenate([rot.astype(x.dtype), x[..., ROPE_DIMS:]], axis=-1)


def _masked_softmax(s, mask):
    p = jax.nn.softmax(jnp.where(mask, s, NEG), axis=-1)
    return jnp.where(mask, p, 0.0)


def _q_blocks(tq):
    qb = min(Q_BLOCK, tq)
    nb = -(-tq // qb)
    return qb, nb, nb * qb


def _mixer_inputs(x, pos, norm1_w, w_in, g_nsa_q, g_cmp_k, g_sel_k, g_win_k, g_mem_q):
    B, T, _ = x.shape
    h = _rmsnorm(x, norm1_w)
    z = h @ w_in
    cuts = np.cumsum(COL_SPLITS)[:-1].tolist()
    (q, kc, vc, ks, vs, kw, vw, ng, sq, sk, sv, mq, bg) = jnp.split(z, cuts, axis=-1)
    hd = lambda a, n: a.reshape(B, T, n, HEAD_DIM)
    q = _rope(_rmsnorm(hd(q, NSA_HEADS), g_nsa_q), pos)
    kc = _rope(_rmsnorm(hd(kc, NSA_KV), g_cmp_k), pos)
    ks = _rope(_rmsnorm(hd(ks, NSA_KV), g_sel_k), pos)
    kw = _rope(_rmsnorm(hd(kw, NSA_KV), g_win_k), pos)
    mq = _rmsnorm(hd(mq, MEM_HEADS), g_mem_q)
    return (q, kc, hd(vc, NSA_KV), ks, hd(vs, NSA_KV), kw, hd(vw, NSA_KV),
            ng.reshape(B, T, NSA_HEADS, 3), hd(sq, SB_HEADS), hd(sk, SB_KV), hd(sv, SB_KV),
            mq, bg.reshape(B, T, N_BRANCH, D_MODEL))


def _compress(k, pos_emb, w1, w2):
    B, L, G, dh = k.shape
    n = L // CMP_BLOCK
    blk = k.reshape(B, n, CMP_BLOCK, G, dh) + pos_emb[None, None, :, None, :]
    flat = blk.transpose(0, 1, 3, 2, 4).reshape(B, n, G, CMP_BLOCK * dh)
    return jax.nn.gelu(flat @ w1) @ w2


def _nsa_attend(q, gate, q_pos0, kc, vc, ks, vs, kw_p, vw_p,
                cmp_pos_k, cmp_k_w1, cmp_k_w2, cmp_pos_v, cmp_v_w1, cmp_v_w2):
    B, Tq, H, dh = q.shape
    G = kc.shape[2]
    R = H // G
    Tk = kc.shape[1]
    scale = HEAD_DIM ** -0.5
    tk_pad = -(-Tk // SEL_BLOCK) * SEL_BLOCK
    kpad = ((0, 0), (0, tk_pad - Tk), (0, 0), (0, 0))
    kc, vc, ks, vs = [jnp.pad(a, kpad) for a in (kc, vc, ks, vs)]
    k_cmp = _compress(kc, cmp_pos_k, cmp_k_w1, cmp_k_w2)
    v_cmp = _compress(vc, cmp_pos_v, cmp_v_w1, cmp_v_w2)
    n_cmp = tk_pad // CMP_BLOCK
    n_sel = tk_pad // SEL_BLOCK
    k_top = min(N_SEL, n_sel)
    ks_blk = ks.reshape(B, n_sel, SEL_BLOCK, G, dh).transpose(0, 3, 1, 2, 4)
    vs_blk = vs.reshape(B, n_sel, SEL_BLOCK, G, dh).transpose(0, 3, 1, 2, 4)
    qb_len, nb, tq_pad = _q_blocks(Tq)
    qpad = ((0, 0), (0, tq_pad - Tq), (0, 0), (0, 0))
    q = jnp.pad(q, qpad)
    gate = jnp.pad(gate, qpad)
    kw_p = jnp.pad(kw_p, qpad)
    vw_p = jnp.pad(vw_p, qpad)
    cmp_end = (jnp.arange(n_cmp) + 1) * CMP_BLOCK - 1
    sel_ids = jnp.arange(n_sel)
    sel_start = sel_ids * SEL_BLOCK
    bi = jnp.arange(B)[:, None, None, None]
    gi = jnp.arange(G)[None, None, :, None]

    def body(b):
        start = b * qb_len
        qb = lax.dynamic_slice_in_dim(q, start, qb_len, axis=1).reshape(B, qb_len, G, R, dh)
        gb = jax.nn.sigmoid(lax.dynamic_slice_in_dim(gate, start, qb_len, axis=1).astype(jnp.float32))
        tpos = q_pos0 + start + jnp.arange(qb_len)
        sc = jnp.einsum('bqgrd,bngd->bqgrn', qb, k_cmp).astype(jnp.float32) * scale
        mc = (cmp_end[None, :] <= tpos[:, None])[None, :, None, None, :]
        pc = _masked_softmax(sc, mc)
        o_c = jnp.einsum('bqgrn,bngd->bqgrd', pc.astype(v_cmp.dtype), v_cmp)
        imp = pc.sum(axis=3).reshape(B, qb_len, G, n_sel, SEL_BLOCK // CMP_BLOCK).sum(-1)
        cur = tpos // SEL_BLOCK
        forced = ((sel_ids[None, :] == 0) | (sel_ids[None, :] == cur[:, None])
                  | (sel_ids[None, :] == cur[:, None] - 1)).astype(jnp.float32)
        valid = sel_start[None, :] <= tpos[:, None]
        score = jnp.where(valid[None, :, None, :], imp + FORCE_BONUS * forced[None, :, None, :], NEG)
        top, idx = lax.top_k(score, k_top)
        ok = top > NEG / 2
        kg = ks_blk[bi, gi, idx]
        vg = vs_blk[bi, gi, idx]
        tokpos = idx[..., None] * SEL_BLOCK + jnp.arange(SEL_BLOCK)
        ms = ok[..., None] & (tokpos <= tpos[None, :, None, None, None])
        ss = jnp.einsum('bqgrd,bqgksd->bqgrks', qb, kg).astype(jnp.float32) * scale
        ps = _masked_softmax(ss.reshape(B, qb_len, G, R, k_top * SEL_BLOCK),
                             ms.reshape(B, qb_len, G, 1, k_top * SEL_BLOCK))
        ps = ps.reshape(B, qb_len, G, R, k_top, SEL_BLOCK).astype(vg.dtype)
        o_s = jnp.einsum('bqgrks,bqgksd->bqgrd', ps, vg)
        kwb = lax.dynamic_slice_in_dim(kw_p, start, WINDOW + qb_len, axis=1)
        vwb = lax.dynamic_slice_in_dim(vw_p, start, WINDOW + qb_len, axis=1)
        kpos = q_pos0 - WINDOW + start + jnp.arange(WINDOW + qb_len)
        mw = ((kpos[None, :] <= tpos[:, None]) & (kpos[None, :] > tpos[:, None] - WINDOW)
              & (kpos[None, :] >= 0))
        sw = jnp.einsum('bqgrd,bkgd->bqgrk', qb, kwb).astype(jnp.float32) * scale
        pw = _masked_softmax(sw, mw[None, :, None, None, :])
        o_w = jnp.einsum('bqgrk,bkgd->bqgrd', pw.astype(vwb.dtype), vwb)
        o = (gb[..., 0:1] * o_c.reshape(B, qb_len, H, dh).astype(jnp.float32)
             + gb[..., 1:2] * o_s.reshape(B, qb_len, H, dh).astype(jnp.float32)
             + gb[..., 2:3] * o_w.reshape(B, qb_len, H, dh).astype(jnp.float32))
        return o.astype(q.dtype)

    out = lax.map(body, jnp.arange(nb))
    return out.transpose(1, 0, 2, 3, 4).reshape(B, tq_pad, H, dh)[:, :Tq]


def _stick_breaking(q, k, v, q_pos0):
    B, Tq, H, dh = q.shape
    G = k.shape[2]
    R = H // G
    Tk = k.shape[1]
    scale = HEAD_DIM ** -0.5
    qb_len, nb, tq_pad = _q_blocks(Tq)
    q = jnp.pad(q, ((0, 0), (0, tq_pad - Tq), (0, 0), (0, 0)))
    kpos = jnp.arange(Tk)

    def body(b):
        start = b * qb_len
        qb = lax.dynamic_slice_in_dim(q, start, qb_len, axis=1).reshape(B, qb_len, G, R, dh)
        tpos = q_pos0 + start + jnp.arange(qb_len)
        z = jnp.einsum('bqgrd,bkgd->bqgrk', qb, k).astype(jnp.float32) * scale
        causal = (kpos[None, :] < tpos[:, None])[None, :, None, None, :]
        log_beta = jax.nn.log_sigmoid(z)
        log_keep = jnp.where(causal, jax.nn.log_sigmoid(-z), 0.0)
        suffix = lax.cumsum(log_keep, axis=4, reverse=True)
        after = jnp.concatenate([suffix[..., 1:], jnp.zeros_like(suffix[..., :1])], axis=-1)
        a = jnp.where(causal, jnp.exp(log_beta + after), 0.0)
        o = jnp.einsum('bqgrk,bkgd->bqgrd', a.astype(v.dtype), v)
        return o.reshape(B, qb_len, H, dh)

    out = lax.map(body, jnp.arange(nb))
    return out.transpose(1, 0, 2, 3, 4).reshape(B, tq_pad, H, dh)[:, :Tq]


def _mem_kv(mem, mem_norm_w, w_mem_kv, g_mem_k):
    B, N, _ = mem.shape
    kv = _rmsnorm(mem, mem_norm_w) @ w_mem_kv
    k, v = jnp.split(kv, 2, axis=-1)
    k = _rmsnorm(k.reshape(B, N, MEM_HEADS, HEAD_DIM), g_mem_k)
    return k, v.reshape(B, N, MEM_HEADS, HEAD_DIM)


def _mem_attend(q, k, v):
    s = jnp.einsum('bthd,bnhd->bthn', q, k).astype(jnp.float32) * (HEAD_DIM ** -0.5)
    p = jax.nn.softmax(s, axis=-1)
    return jnp.einsum('bthn,bnhd->bthd', p.astype(v.dtype), v)


def _merge_ffn(x, o_nsa, o_sb, o_mem, bgate, w_up_nsa, w_up_sb, w_up_mem, w_o, norm2_w, w_ff1, w_ff2):
    B, T, _ = x.shape
    g = jax.nn.sigmoid(bgate.astype(jnp.float32)).astype(x.dtype)
    u = (g[:, :, 0] * (o_nsa.reshape(B, T, NSA_QW) @ w_up_nsa)
         + g[:, :, 1] * (o_sb.reshape(B, T, SB_QW) @ w_up_sb)
         + g[:, :, 2] * (o_mem.reshape(B, T, MEM_W) @ w_up_mem))
    x = x + u @ w_o
    h2 = _rmsnorm(x, norm2_w)
    return x + jnp.square(jax.nn.relu(h2 @ w_ff1)) @ w_ff2


def _prompt_layer(x, mem, proj_w, cmp_w, mem_w, out_w):
    B, T, _ = x.shape
    pos = jnp.arange(T, dtype=jnp.int32)
    (q, kc, vc, ks, vs, kw, vw, ng, sq, sk, sv, mq, bg) = _mixer_inputs(x, pos, *proj_w)
    pad_w = ((0, 0), (WINDOW, 0), (0, 0), (0, 0))
    o_nsa = _nsa_attend(q, ng, 0, kc, vc, ks, vs, jnp.pad(kw, pad_w), jnp.pad(vw, pad_w), *cmp_w)
    o_sb = _stick_breaking(sq, sk, sv, 0)
    mk, mv = _mem_kv(mem, *mem_w)
    o_mem = _mem_attend(mq, mk, mv)
    y = _merge_ffn(x, o_nsa, o_sb, o_mem, bg, *out_w)
    keep = min(WINDOW, T)
    return y, (kc, vc, ks, vs, sk, sv, kw[:, T - keep:], vw[:, T - keep:], mk, mv)


def _sample_layer(x, cache_cmp_k, cache_cmp_v, cache_sel_k, cache_sel_v, cache_sb_k, cache_sb_v,
                  state_win_k, state_win_v, cache_mem_k, cache_mem_v, page_table, proj_w, cmp_w, out_w):
    B, T, _ = x.shape
    past = page_table.shape[1] * cache_cmp_k.shape[1]
    pos = past + jnp.arange(T, dtype=jnp.int32)
    (q, kc, vc, ks, vs, kw, vw, ng, sq, sk, sv, mq, bg) = _mixer_inputs(x, pos, *proj_w)

    def with_past(cache, new):
        old = cache[page_table].reshape(B, past, cache.shape[2], cache.shape[3])
        return jnp.concatenate([old, new], axis=1)

    lb = state_win_k.shape[1]
    front = ((0, 0), (WINDOW - lb, 0), (0, 0), (0, 0))
    win_k = jnp.concatenate([state_win_k, kw], axis=1)
    win_v = jnp.concatenate([state_win_v, vw], axis=1)
    o_nsa = _nsa_attend(q, ng, past, with_past(cache_cmp_k, kc), with_past(cache_cmp_v, vc),
                        with_past(cache_sel_k, ks), with_past(cache_sel_v, vs),
                        jnp.pad(win_k, front), jnp.pad(win_v, front), *cmp_w)
    o_sb = _stick_breaking(sq, with_past(cache_sb_k, sk), with_past(cache_sb_v, sv), past)
    o_mem = _mem_attend(mq, cache_mem_k, cache_mem_v)
    y = _merge_ffn(x, o_nsa, o_sb, o_mem, bg, *out_w)
    keep = min(WINDOW, past + T)
    return y, (kc, vc, ks, vs, sk, sv, win_k[:, lb + T - keep:], win_v[:, lb + T - keep:])


def setup_inputs(seed: int = 0) -> dict:
    key = jax.random.key(seed)
    ks = jax.random.split(key, 40)
    n_pages = PAST_LEN // PAGE_SIZE
    n_used = DEC_BATCH * n_pages
    n_phys = n_used + max(1, n_used // 4)
    f32 = jnp.float32

    def nrm(k, shape, scale=1.0):
        return jax.random.normal(k, shape, f32) * scale

    def gain(k, n):
        return 1.0 + 0.02 * jax.random.normal(k, (n,), f32)

    paged = (n_phys, PAGE_SIZE, NSA_KV, HEAD_DIM)
    win_len = min(WINDOW, PAST_LEN)
    page_table = jax.random.permutation(ks[10], n_phys)[:n_used].reshape(DEC_BATCH, n_pages).astype(jnp.int32)
    return {
        "x_prompt": nrm(ks[0], (BATCH, SEQ, D_MODEL)),
        "mem_prompt": nrm(ks[1], (BATCH, N_MEM, D_MODEL)),
        "x_sample": nrm(ks[2], (DEC_BATCH, DEC_SEQ, D_MODEL)),
        "cache_cmp_k": nrm(ks[3], paged),
        "cache_cmp_v": nrm(ks[4], paged),
        "cache_sel_k": nrm(ks[5], paged),
        "cache_sel_v": nrm(ks[6], paged),
        "cache_sb_k": nrm(ks[7], (n_phys, PAGE_SIZE, SB_KV, HEAD_DIM)),
        "cache_sb_v": nrm(ks[8], (n_phys, PAGE_SIZE, SB_KV, HEAD_DIM)),
        "state_win_k": nrm(ks[9], (DEC_BATCH, win_len, NSA_KV, HEAD_DIM)),
        "state_win_v": nrm(ks[11], (DEC_BATCH, win_len, NSA_KV, HEAD_DIM)),
        "cache_mem_k": nrm(ks[12], (DEC_BATCH, N_MEM, MEM_HEADS, HEAD_DIM)),
        "cache_mem_v": nrm(ks[13], (DEC_BATCH, N_MEM, MEM_HEADS, HEAD_DIM)),
        "page_table": page_table,
        "norm1_w": gain(ks[14], D_MODEL),
        "w_in": nrm(ks[15], (D_MODEL, IN_COLS), D_MODEL ** -0.5),
        "g_nsa_q": gain(ks[16], HEAD_DIM),
        "g_cmp_k": gain(ks[17], HEAD_DIM),
        "g_sel_k": gain(ks[18], HEAD_DIM),
        "g_win_k": gain(ks[19], HEAD_DIM),
        "g_mem_q": gain(ks[20], HEAD_DIM),
        "g_mem_k": gain(ks[21], HEAD_DIM),
        "mem_norm_w": gain(ks[22], D_MODEL),
        "w_mem_kv": nrm(ks[23], (D_MODEL, 2 * MEM_W), D_MODEL ** -0.5),
        "cmp_pos_k": nrm(ks[24], (CMP_BLOCK, HEAD_DIM), 0.02),
        "cmp_k_w1": nrm(ks[25], (CMP_BLOCK * HEAD_DIM, CMP_HIDDEN), (CMP_BLOCK * HEAD_DIM) ** -0.5),
        "cmp_k_w2": nrm(ks[26], (CMP_HIDDEN, HEAD_DIM), CMP_HIDDEN ** -0.5),
        "cmp_pos_v": nrm(ks[27], (CMP_BLOCK, HEAD_DIM), 0.02),
        "cmp_v_w1": nrm(ks[28], (CMP_BLOCK * HEAD_DIM, CMP_HIDDEN), (CMP_BLOCK * HEAD_DIM) ** -0.5),
        "cmp_v_w2": nrm(ks[29], (CMP_HIDDEN, HEAD_DIM), CMP_HIDDEN ** -0.5),
        "w_up_nsa": nrm(ks[30], (NSA_QW, D_MODEL), NSA_QW ** -0.5),
        "w_up_sb": nrm(ks[31], (SB_QW, D_MODEL), SB_QW ** -0.5),
        "w_up_mem": nrm(ks[32], (MEM_W, D_MODEL), MEM_W ** -0.5),
        "w_o": nrm(ks[33], (D_MODEL, D_MODEL), D_MODEL ** -0.5),
        "norm2_w": gain(ks[34], D_MODEL),
        "w_ff1": nrm(ks[35], (D_MODEL, D_FF), D_MODEL ** -0.5),
        "w_ff2": nrm(ks[36], (D_FF, D_MODEL), D_FF ** -0.5),
    }


def reference(x_prompt, mem_prompt, x_sample, cache_cmp_k, cache_cmp_v, cache_sel_k, cache_sel_v,
              cache_sb_k, cache_sb_v, state_win_k, state_win_v, cache_mem_k, cache_mem_v, page_table,
              norm1_w, w_in, g_nsa_q, g_cmp_k, g_sel_k, g_win_k, g_mem_q, g_mem_k, mem_norm_w, w_mem_kv,
              cmp_pos_k, cmp_k_w1, cmp_k_w2, cmp_pos_v, cmp_v_w1, cmp_v_w2,
              w_up_nsa, w_up_sb, w_up_mem, w_o, norm2_w, w_ff1, w_ff2):
    proj_w = (norm1_w, w_in, g_nsa_q, g_cmp_k, g_sel_k, g_win_k, g_mem_q)
    cmp_w = (cmp_pos_k, cmp_k_w1, cmp_k_w2, cmp_pos_v, cmp_v_w1, cmp_v_w2)
    mem_w = (mem_norm_w, w_mem_kv, g_mem_k)
    out_w = (w_up_nsa, w_up_sb, w_up_mem, w_o, norm2_w, w_ff1, w_ff2)
    y_prompt = x_prompt
    y_sample = x_sample
    for _ in range(DEPTH):
        y_prompt, (p_cmp_k, p_cmp_v, p_sel_k, p_sel_v, p_sb_k, p_sb_v,
                   p_win_k, p_win_v, p_mem_k, p_mem_v) = _prompt_layer(
            y_prompt, mem_prompt, proj_w, cmp_w, mem_w, out_w)
        y_sample, (s_cmp_k, s_cmp_v, s_sel_k, s_sel_v, s_sb_k, s_sb_v,
                   s_win_k, s_win_v) = _sample_layer(
            y_sample, cache_cmp_k, cache_cmp_v, cache_sel_k, cache_sel_v, cache_sb_k, cache_sb_v,
            state_win_k, state_win_v, cache_mem_k, cache_mem_v, page_table, proj_w, cmp_w, out_w)
    return (y_prompt, y_sample, p_cmp_k, p_cmp_v, p_sel_k, p_sel_v, p_sb_k, p_sb_v, p_win_k, p_win_v,
            p_mem_k, p_mem_v, s_cmp_k, s_cmp_v, s_sel_k, s_sel_v, s_sb_k, s_sb_v, s_win_k, s_win_v)
```

```python
import functools
import math

import numpy as np
import jax
import jax.numpy as jnp
from jax import lax
from jax.experimental import pallas as pl
from jax.experimental.pallas import tpu as pltpu

F32 = jnp.float32
BF16 = jnp.bfloat16

D_MODEL = 4096
HEAD_DIM = 128
NSA_HEADS = 16
NSA_KV = 4
SB_HEADS = 12
MEM_HEADS = 4
CMP_BLOCK = 32
SEL_BLOCK = 64
N_SEL = 16
WINDOW = 512
PAGE = 128
ROPE_THETA = 500000.0
ROPE_DIMS = HEAD_DIM // 4
EPS = 1e-6
NEG = -1e30
M_FLOOR = -1e20
FORCE_BONUS = 1e4
SCALE = HEAD_DIM ** -0.5
KVW = NSA_KV * HEAD_DIM

VMEM_LIMIT = 56 * 1024 * 1024


def _row_tile(m, cap):
    for step in (16, 8):
        best = max((t for t in range(step, cap + 1, step) if m % t == 0), default=0)
        if best:
            return best
    raise ValueError(f"no row tile for {m}")


def _cparams(sem, vmem=VMEM_LIMIT):
    return pltpu.CompilerParams(dimension_semantics=sem, vmem_limit_bytes=vmem)


def _dot(a, b):
    return jnp.dot(a, b, preferred_element_type=F32)


def _dot_nt(a, b):
    return lax.dot_general(a, b, (((1,), (1,)), ((), ())), preferred_element_type=F32)


def _iota(shape, dim):
    return lax.broadcasted_iota(jnp.int32, shape, dim)


def _rms2_kernel(xp_ref, xs_ref, w_ref, o_ref, *, n_prompt_tiles):
    i = pl.program_id(0)
    x = jnp.where(i < n_prompt_tiles, xp_ref[...], xs_ref[...])
    ms = jnp.mean(x * x, axis=-1, keepdims=True)
    o_ref[...] = (x * lax.rsqrt(ms + EPS) * w_ref[...]).astype(o_ref.dtype)


def _rms_stack(xp, xs, w):
    tp, d = xp.shape
    ts = xs.shape[0]
    assert tp % ts == 0 and ts % 8 == 0
    npt = tp // ts
    return pl.pallas_call(
        functools.partial(_rms2_kernel, n_prompt_tiles=npt),
        grid=(npt + 1,),
        in_specs=[pl.BlockSpec((ts, d), lambda i: (jnp.minimum(i, npt - 1), 0)),
                  pl.BlockSpec((ts, d), lambda i: (0, 0)),
                  pl.BlockSpec((1, d), lambda i: (0, 0))],
        out_specs=pl.BlockSpec((ts, d), lambda i: (i, 0)),
        out_shape=jax.ShapeDtypeStruct((tp + ts, d), BF16),
        compiler_params=_cparams(("arbitrary",)),
        name="rms_stack",
    )(xp, xs, w.reshape(1, d))


def _rms_kernel(x_ref, w_ref, o_ref):
    x = x_ref[...]
    ms = jnp.mean(x * x, axis=-1, keepdims=True)
    o_ref[...] = (x * lax.rsqrt(ms + EPS) * w_ref[...]).astype(o_ref.dtype)


def _rms(x, w, tm):
    m, d = x.shape
    return pl.pallas_call(
        _rms_kernel, grid=(m // tm,),
        in_specs=[pl.BlockSpec((tm, d), lambda i: (i, 0)), pl.BlockSpec((1, d), lambda i: (0, 0))],
        out_specs=pl.BlockSpec((tm, d), lambda i: (i, 0)),
        out_shape=jax.ShapeDtypeStruct((m, d), BF16),
        compiler_params=_cparams(("arbitrary",)),
        name="rms",
    )(x, w.reshape(1, d))


MODE_PLAIN, MODE_NORM_ROPE, MODE_NORM, MODE_SIGMOID = 0, 1, 2, 3
IN_TN = 512


def _head_norm(z, g):
    outs = []
    for hh in range(z.shape[1] // HEAD_DIM):
        xs = z[:, hh * HEAD_DIM:(hh + 1) * HEAD_DIM]
        ms = jnp.mean(xs * xs, axis=-1, keepdims=True)
        outs.append(xs * lax.rsqrt(ms + EPS) * g)
    return outs


def _rope(x, c, s_lo, s_hi):
    half = ROPE_DIMS // 2
    return x * c + pltpu.roll(x, HEAD_DIM - half, 1) * s_lo + pltpu.roll(x, half, 1) * s_hi


def _inproj_kernel(mode_ref, gidx_ref, a_ref, b_ref, g_ref, c_ref, slo_ref, shi_ref, o_ref):
    j = pl.program_id(1)
    mode = mode_ref[j]
    o_ref[...] = _dot(a_ref[...], b_ref[...])

    @pl.when(mode == MODE_SIGMOID)
    def _():
        o_ref[...] = jax.nn.sigmoid(o_ref[...])

    @pl.when(mode == MODE_NORM)
    def _():
        o_ref[...] = jnp.concatenate(_head_norm(o_ref[...], g_ref[0]), axis=1)

    @pl.when(mode == MODE_NORM_ROPE)
    def _():
        c, slo, shi = c_ref[...], slo_ref[...], shi_ref[...]
        o_ref[...] = jnp.concatenate([_rope(y, c, slo, shi) for y in _head_norm(o_ref[...], g_ref[0])], axis=1)


def _inproj(h, w, modes, gidx, gains, rope_c, rope_lo, rope_hi, tm):
    m, k = h.shape
    n = w.shape[1]
    nj = n // IN_TN
    grid_spec = pltpu.PrefetchScalarGridSpec(
        num_scalar_prefetch=2, grid=(m // tm, nj),
        in_specs=[pl.BlockSpec((tm, k), lambda i, j, *_: (i, 0)),
                  pl.BlockSpec((k, IN_TN), lambda i, j, *_: (0, j)),
                  pl.BlockSpec((1, 1, HEAD_DIM), lambda i, j, mo, gi: (gi[j], 0, 0)),
                  pl.BlockSpec((tm, HEAD_DIM), lambda i, j, *_: (i, 0)),
                  pl.BlockSpec((tm, HEAD_DIM), lambda i, j, *_: (i, 0)),
                  pl.BlockSpec((tm, HEAD_DIM), lambda i, j, *_: (i, 0))],
        out_specs=pl.BlockSpec((tm, IN_TN), lambda i, j, *_: (i, j)))
    return pl.pallas_call(
        _inproj_kernel, grid_spec=grid_spec,
        out_shape=jax.ShapeDtypeStruct((m, n), F32),
        compiler_params=_cparams(("arbitrary", "arbitrary")),
        name="inproj",
    )(modes, gidx, h, w, gains, rope_c, rope_lo, rope_hi)


def _mm_kernel(*refs, nk, n_extra, epi):
    a_ref, b_ref = refs[0], refs[1]
    extra = refs[2:2 + n_extra]
    o_ref = refs[2 + n_extra]
    part = _dot(a_ref[...], b_ref[...])
    if nk == 1:
        o_ref[...] = epi(part, *[e[...] for e in extra]).astype(o_ref.dtype)
    else:
        acc_ref = refs[3 + n_extra]
        kk = pl.program_id(2)

        @pl.when(kk == 0)
        def _():
            acc_ref[...] = part

        @pl.when(kk > 0)
        def _():
            acc_ref[...] += part

        @pl.when(kk == nk - 1)
        def _():
            o_ref[...] = epi(acc_ref[...], *[e[...] for e in extra]).astype(o_ref.dtype)


def _mm(a, b, *, tm, tn, tk, out_dtype, epi=None, extras=(), col0=0, ncols=None, name="mm"):
    m, k = a.shape
    ncols = b.shape[1] - col0 * tn if ncols is None else ncols
    assert m % tm == 0 and ncols % tn == 0 and k % tk == 0
    nk = k // tk
    if epi is None:
        epi = lambda z: z
    in_specs = [pl.BlockSpec((tm, tk), lambda i, j, kk: (i, kk)),
                pl.BlockSpec((tk, tn), lambda i, j, kk: (kk, j + col0))]
    args = [a, b]
    for arr, off in extras:
        if arr.shape[0] == 1:
            in_specs.append(pl.BlockSpec((1, arr.shape[1]), lambda i, j, kk: (0, 0)))
        else:
            in_specs.append(pl.BlockSpec((tm, tn), lambda i, j, kk, off=off: (i, j + off)))
        args.append(arr)
    scratch = [pltpu.VMEM((tm, tn), F32)] if nk > 1 else []
    return pl.pallas_call(
        functools.partial(_mm_kernel, nk=nk, n_extra=len(extras), epi=epi),
        grid=(m // tm, ncols // tn, nk),
        in_specs=in_specs,
        out_specs=pl.BlockSpec((tm, tn), lambda i, j, kk: (i, j)),
        out_shape=jax.ShapeDtypeStruct((m, ncols), out_dtype),
        scratch_shapes=scratch,
        compiler_params=_cparams(("arbitrary", "arbitrary", "arbitrary")),
        name=name,
    )(*args)


def _epi_headnorm(z, g):
    return jnp.concatenate(_head_norm(z, g), axis=1)


def _epi_sigmoid(z):
    return jax.nn.sigmoid(z)


def _epi_residual(z, x):
    return x + z


def _epi_relu2(z):
    return jnp.square(jnp.maximum(z, 0.0))


def _up_kernel(on_ref, os_ref, om_ref, wn_ref, ws_ref, wm_ref, g0_ref, g1_ref, g2_ref, o_ref):
    u = (g0_ref[...] * _dot(on_ref[...], wn_ref[...])
         + g1_ref[...] * _dot(os_ref[...], ws_ref[...])
         + g2_ref[...] * _dot(om_ref[...], wm_ref[...]))
    o_ref[...] = u.astype(o_ref.dtype)


def _merge_up(o_nsa, o_sb, o_mem, w_n, w_s, w_m, z, gate_col0, tm, tn):
    m = o_nsa.shape[0]
    d = w_n.shape[1]
    gb = gate_col0 // tn
    nd = d // tn
    row = lambda kdim: pl.BlockSpec((tm, kdim), lambda i, j: (i, 0))
    col = lambda kdim: pl.BlockSpec((kdim, tn), lambda i, j: (0, j))
    gate = lambda c: pl.BlockSpec((tm, tn), lambda i, j, c=c: (i, gb + c * nd + j))
    return pl.pallas_call(
        _up_kernel, grid=(m // tm, nd),
        in_specs=[row(o_nsa.shape[1]), row(o_sb.shape[1]), row(o_mem.shape[1]),
                  col(w_n.shape[0]), col(w_s.shape[0]), col(w_m.shape[0]),
                  gate(0), gate(1), gate(2)],
        out_specs=pl.BlockSpec((tm, tn), lambda i, j: (i, j)),
        out_shape=jax.ShapeDtypeStruct((m, d), BF16),
        compiler_params=_cparams(("arbitrary", "arbitrary")),
        name="merge_up",
    )(o_nsa, o_sb, o_mem, w_n, w_s, w_m, z, z, z)


CMP_PAGES = 16
CMP_HID = 256


def _gelu_tanh(x):
    cdf = 0.5 * (1.0 + jnp.tanh(math.sqrt(2.0 / math.pi) * (x + 0.044715 * (x * x * x))))
    return x * cdf


def _compress_kernel(pt_ref, *refs):
    del pt_ref
    page_refs = refs[:CMP_PAGES]
    pos_ref, w1_ref, w2_ref, o_ref, buf_ref = refs[CMP_PAGES:]
    for c in range(CMP_PAGES):
        for g in range(NSA_KV):
            buf_ref[g, c * PAGE:(c + 1) * PAGE, :] = page_refs[c][0, :, g * HEAD_DIM:(g + 1) * HEAD_DIM]
    nrow = CMP_PAGES * (PAGE // CMP_BLOCK)
    acc = jnp.zeros((NSA_KV * nrow, CMP_HID), F32)
    for t in range(CMP_BLOCK):
        parts = [buf_ref[g, pl.ds(t, nrow, stride=CMP_BLOCK), :] for g in range(NSA_KV)]
        x = jnp.concatenate(parts, axis=0) + pos_ref[t:t + 1, :]
        acc = acc + _dot(x.astype(BF16), w1_ref[t * HEAD_DIM:(t + 1) * HEAD_DIM, :])
    out = _dot(_gelu_tanh(acc).astype(BF16), w2_ref[...])
    for g in range(NSA_KV):
        o_ref[0, :, g * HEAD_DIM:(g + 1) * HEAD_DIM] = out[g * nrow:(g + 1) * nrow, :]


def _compress(pages, page_ids, pos, w1, w2):
    n_steps = page_ids.shape[0] // CMP_PAGES
    nrow = CMP_PAGES * (PAGE // CMP_BLOCK)
    page_specs = [pl.BlockSpec((1, PAGE, KVW), lambda s, pt, c=c: (pt[s * CMP_PAGES + c], 0, 0))
                  for c in range(CMP_PAGES)]
    grid_spec = pltpu.PrefetchScalarGridSpec(
        num_scalar_prefetch=1, grid=(n_steps,),
        in_specs=page_specs + [pl.BlockSpec(pos.shape, lambda s, pt: (0, 0)),
                               pl.BlockSpec(w1.shape, lambda s, pt: (0, 0)),
                               pl.BlockSpec(w2.shape, lambda s, pt: (0, 0))],
        out_specs=pl.BlockSpec((1, nrow, KVW), lambda s, pt: (s, 0, 0)),
        scratch_shapes=[pltpu.VMEM((NSA_KV, CMP_PAGES * PAGE, HEAD_DIM), F32)])
    return pl.pallas_call(
        _compress_kernel, grid_spec=grid_spec,
        out_shape=jax.ShapeDtypeStruct((n_steps, nrow, KVW), F32),
        compiler_params=_cparams(("arbitrary",)),
        name="compress",
    )(page_ids, *([pages] * CMP_PAGES), pos, w1, w2)


def _softmax_masked(s, mask):
    sm = jnp.where(mask, s, NEG)
    e = jnp.exp(sm - jnp.max(sm, axis=-1, keepdims=True))
    p = e * (1.0 / jnp.sum(e, axis=-1, keepdims=True))
    return jnp.where(mask, p, 0.0)


def _cmp_block_end(n_lanes):
    half = n_lanes // 2
    lane = _iota((1, n_lanes), 1)
    blk = jnp.where(lane < half, 2 * lane, 2 * (lane - half) + 1)
    return (blk + 1) * CMP_BLOCK - 1


def _top_blocks(score, k_top):
    nb = score.shape[1]
    lane = _iota(score.shape, 1).astype(F32)
    work = score
    picked = jnp.zeros(score.shape, jnp.bool_)
    for _ in range(k_top):
        mx = jnp.max(work, axis=1, keepdims=True)
        first = jnp.min(jnp.where(work == mx, lane, float(nb)), axis=1, keepdims=True)
        hit = lane == first
        picked = picked | hit
        work = jnp.where(hit, -jnp.inf, work)
    return picked


def _select_mask(imp, tpos):
    nb = imp.shape[1]
    b = _iota((1, nb), 1)
    cur = tpos >> 6
    forced = (b == 0) | (b == cur) | (b == cur - 1)
    valid = (b * SEL_BLOCK) <= tpos
    score = jnp.where(valid, imp + FORCE_BONUS * forced.astype(F32), NEG)
    return _top_blocks(score, min(N_SEL, nb)) & valid


NSA_TQ = 128
NSA_TK = 512
NSA_R = NSA_HEADS // NSA_KV


def _nsa_prompt_kernel(q_ref, gt_ref, kc_ref, vc_ref, ka_ref, vs_ref, kw_ref, vw_ref, o_ref,
                       m_ref, l_ref, acc_ref):
    i = pl.program_id(1)
    s0 = i * NSA_TQ
    rows = NSA_R * NSA_TQ
    q = q_ref[...]
    qr = jnp.concatenate([q[:, r * HEAD_DIM:(r + 1) * HEAD_DIM] for r in range(NSA_R)], axis=0).astype(BF16)
    tpos_q = s0 + _iota((NSA_TQ, 1), 0)
    tpos = jnp.concatenate([tpos_q] * NSA_R, axis=0)

    nc = kc_ref.shape[0]
    nb = nc // 2
    sc = _dot_nt(qr, kc_ref[...].astype(BF16)) * SCALE
    pc = _softmax_masked(sc, _cmp_block_end(nc) <= tpos)
    o_c = _dot(pc.astype(BF16), vc_ref[...].astype(BF16))
    pcs = pc[0:NSA_TQ]
    for r in range(1, NSA_R):
        pcs = pcs + pc[r * NSA_TQ:(r + 1) * NSA_TQ]
    imp = pcs[:, :nb] + pcs[:, nb:]

    sel = _select_mask(imp, tpos_q)
    bias = jnp.where(sel, 0.0, NEG).astype(BF16)
    q_aug = jnp.concatenate([qr, jnp.concatenate([bias] * NSA_R, axis=0)], axis=1)
    m_ref[...] = jnp.full((rows, 1), M_FLOOR, F32)
    l_ref[...] = jnp.zeros((rows, 1), F32)
    acc_ref[...] = jnp.zeros((rows, HEAD_DIM), F32)

    def body(kt, carry):
        k0 = pl.multiple_of(kt * NSA_TK, NSA_TK)
        s = _dot_nt(q_aug, ka_ref[0, pl.ds(k0, NSA_TK), :]) * SCALE
        kpos = k0 + _iota((1, NSA_TK), 1)
        s = jnp.where(kpos <= tpos, s, NEG)
        m_old = m_ref[...]
        m_new = jnp.maximum(m_old, jnp.max(s, axis=1, keepdims=True))
        p = jnp.exp(s - m_new)
        alpha = jnp.exp(m_old - m_new)
        l_ref[...] = alpha * l_ref[...] + jnp.sum(p, axis=1, keepdims=True)
        acc_ref[...] = alpha * acc_ref[...] + _dot(p.astype(BF16), vs_ref[pl.ds(k0, NSA_TK), :])
        m_ref[...] = m_new
        return carry

    lax.fori_loop(0, (s0 + NSA_TQ + NSA_TK - 1) >> 9, body, 0)
    o_s = acc_ref[...] * (1.0 / l_ref[...])

    wlen = WINDOW + NSA_TQ
    w0 = pl.multiple_of(jnp.maximum(s0 - WINDOW, 0), NSA_TQ)
    sw = _dot_nt(qr, kw_ref[pl.ds(w0, wlen), :]) * SCALE
    kpos = w0 + _iota((1, wlen), 1)
    pw = _softmax_masked(sw, (kpos <= tpos) & (kpos > tpos - WINDOW))
    o_w = _dot(pw.astype(BF16), vw_ref[pl.ds(w0, wlen), :])

    gt = gt_ref[0]
    outs = []
    for r in range(NSA_R):
        rs = slice(r * NSA_TQ, (r + 1) * NSA_TQ)
        outs.append(gt[:, 3 * r:3 * r + 1] * o_c[rs] + gt[:, 3 * r + 1:3 * r + 2] * o_s[rs]
                    + gt[:, 3 * r + 2:3 * r + 3] * o_w[rs])
    o_ref[...] = jnp.concatenate(outs, axis=1).astype(o_ref.dtype)


def _nsa_prompt(z, gates_t, k_cmp, v_cmp, k_aug, kvb, t, vs_col, kw_col, vw_col):
    assert t % NSA_TK == 0 and t >= WINDOW + NSA_TQ
    nc = k_cmp.shape[0]
    rows = NSA_R * NSA_TQ
    return pl.pallas_call(
        _nsa_prompt_kernel, grid=(NSA_KV, t // NSA_TQ),
        in_specs=[pl.BlockSpec((NSA_TQ, NSA_R * HEAD_DIM), lambda g, i: (i, g)),
                  pl.BlockSpec((1, NSA_TQ, 3 * NSA_R), lambda g, i: (g, i, 0)),
                  pl.BlockSpec((nc, HEAD_DIM), lambda g, i: (0, g)),
                  pl.BlockSpec((nc, HEAD_DIM), lambda g, i: (0, g)),
                  pl.BlockSpec((1, t, 2 * HEAD_DIM), lambda g, i: (g, 0, 0)),
                  pl.BlockSpec((t, HEAD_DIM), lambda g, i: (0, vs_col + g)),
                  pl.BlockSpec((t, HEAD_DIM), lambda g, i: (0, kw_col + g)),
                  pl.BlockSpec((t, HEAD_DIM), lambda g, i: (0, vw_col + g))],
        out_specs=pl.BlockSpec((NSA_TQ, NSA_R * HEAD_DIM), lambda g, i: (i, g)),
        out_shape=jax.ShapeDtypeStruct((t, NSA_HEADS * HEAD_DIM), BF16),
        scratch_shapes=[pltpu.VMEM((rows, 1), F32), pltpu.VMEM((rows, 1), F32),
                        pltpu.VMEM((rows, HEAD_DIM), F32)],
        compiler_params=_cparams(("arbitrary", "arbitrary")),
        name="nsa_prompt",
    )(z, gates_t, k_cmp, v_cmp, k_aug, kvb, kvb, kvb)


SB_T = 256
SB_R = SB_HEADS // NSA_KV


def _strict_upper(n):
    return (_iota((n, n), 0) > _iota((n, n), 1)).astype(BF16)


def _sb_tile(z, causal, carry, u):
    lb = jnp.minimum(z, 0.0) - jnp.log1p(jnp.exp(-jnp.abs(z)))
    lk = lb - z
    if causal is not None:
        lk = jnp.where(causal, lk, 0.0)
    hi = lk.astype(BF16)
    lo = (lk - hi.astype(F32)).astype(BF16)
    after = _dot(hi, u) + _dot(lo, u) + carry
    a = jnp.exp(lb + after)
    if causal is not None:
        a = jnp.where(causal, a, 0.0)
    return a, carry + jnp.sum(lk, axis=1, keepdims=True)


def _sb_prompt_kernel(q_ref, k_ref, v_ref, o_ref, carry_ref, acc_ref):
    i = pl.program_id(1)
    rows = SB_R * SB_T
    q = q_ref[...]
    qr = jnp.concatenate([q[:, r * HEAD_DIM:(r + 1) * HEAD_DIM] for r in range(SB_R)], axis=0)
    u = _strict_upper(SB_T)
    s0 = pl.multiple_of(i * SB_T, SB_T)

    qi = jnp.concatenate([_iota((SB_T, 1), 0)] * SB_R, axis=0)
    causal = _iota((1, SB_T), 1) < qi
    z = _dot_nt(qr, k_ref[pl.ds(s0, SB_T), :]) * SCALE
    a, carry = _sb_tile(z, causal, jnp.zeros((rows, 1), F32), u)
    carry_ref[...] = carry
    acc_ref[...] = _dot(a.astype(BF16), v_ref[pl.ds(s0, SB_T), :])

    def body(kt, c):
        k0 = pl.multiple_of((i - 1 - kt) * SB_T, SB_T)
        zt = _dot_nt(qr, k_ref[pl.ds(k0, SB_T), :]) * SCALE
        at, cn = _sb_tile(zt, None, carry_ref[...], u)
        carry_ref[...] = cn
        acc_ref[...] += _dot(at.astype(BF16), v_ref[pl.ds(k0, SB_T), :])
        return c

    lax.fori_loop(0, i, body, 0)
    acc = acc_ref[...]
    o_ref[...] = jnp.concatenate([acc[r * SB_T:(r + 1) * SB_T] for r in range(SB_R)], axis=1).astype(o_ref.dtype)


def _sb_prompt(sq, sk, sv):
    t = sq.shape[0]
    assert t % SB_T == 0
    rows = SB_R * SB_T
    return pl.pallas_call(
        _sb_prompt_kernel, grid=(NSA_KV, t // SB_T),
        in_specs=[pl.BlockSpec((SB_T, SB_R * HEAD_DIM), lambda g, i: (i, g)),
                  pl.BlockSpec((t, HEAD_DIM), lambda g, i: (0, g)),
                  pl.BlockSpec((t, HEAD_DIM), lambda g, i: (0, g))],
        out_specs=pl.BlockSpec((SB_T, SB_R * HEAD_DIM), lambda g, i: (i, g)),
        out_shape=jax.ShapeDtypeStruct(sq.shape, BF16),
        scratch_shapes=[pltpu.VMEM((rows, 1), F32), pltpu.VMEM((rows, HEAD_DIM), F32)],
        compiler_params=_cparams(("arbitrary", "arbitrary")),
        name="sb_prompt",
    )(sq, sk, sv)


MEM_TQ = 512


def _mem_prompt_kernel(q_ref, k_ref, v_ref, o_ref):
    outs = []
    for h in range(MEM_HEADS):
        hs = slice(h * HEAD_DIM, (h + 1) * HEAD_DIM)
        s = _dot_nt(q_ref[:, hs].astype(BF16), k_ref[:, hs].astype(BF16)) * SCALE
        e = jnp.exp(s - jnp.max(s, axis=-1, keepdims=True))
        p = e * (1.0 / jnp.sum(e, axis=-1, keepdims=True))
        outs.append(_dot(p.astype(BF16), v_ref[:, hs].astype(BF16)))
    o_ref[...] = jnp.concatenate(outs, axis=1).astype(o_ref.dtype)


def _mem_prompt(z, mq_col, mk, mv, t):
    n_mem = mk.shape[0]
    w = MEM_HEADS * HEAD_DIM
    return pl.pallas_call(
        _mem_prompt_kernel, grid=(t // MEM_TQ,),
        in_specs=[pl.BlockSpec((MEM_TQ, w), lambda i: (i, mq_col)),
                  pl.BlockSpec((n_mem, w), lambda i: (0, 0)),
                  pl.BlockSpec((n_mem, w), lambda i: (0, 0))],
        out_specs=pl.BlockSpec((MEM_TQ, w), lambda i: (i, 0)),
        out_shape=jax.ShapeDtypeStruct((t, w), BF16),
        compiler_params=_cparams(("arbitrary",)),
        name="mem_prompt",
    )(z, mk, mv)


HROWS = 16


def _rows_of_group(n_rows, g, heads_per_group):
    row = _iota((n_rows, 1), 0)
    return (row >= g * heads_per_group) & (row < (g + 1) * heads_per_group)


def _group_lanes(full, heads_per_group):
    out = jnp.zeros((full.shape[0], HEAD_DIM), F32)
    for g in range(NSA_KV):
        out = out + jnp.where(_rows_of_group(full.shape[0], g, heads_per_group),
                              full[:, g * HEAD_DIM:(g + 1) * HEAD_DIM], 0.0)
    return out


def _dec_sb_kernel(pt_ref, q_ref, k_ref, v_ref, o_ref, carry_ref, acc_ref):
    del pt_ref
    j = pl.program_id(1)

    @pl.when(j == 0)
    def _():
        carry_ref[...] = jnp.zeros_like(carry_ref)
        acc_ref[...] = jnp.zeros_like(acc_ref)

    z = _dot_nt(q_ref[0], k_ref[0].astype(BF16)) * SCALE
    a, carry = _sb_tile(z, None, carry_ref[...], _strict_upper(PAGE))
    carry_ref[...] = carry
    acc_ref[...] += _dot(a.astype(BF16), v_ref[0].astype(BF16))

    @pl.when(j == pl.num_programs(1) - 1)
    def _():
        o_ref[0] = _group_lanes(acc_ref[...], SB_R).astype(o_ref.dtype)


def _dec_sb(page_ids, q_rows, cache_k, cache_v, n_pages):
    nb = q_rows.shape[0]
    page = lambda b, j, pt: (pt[b * n_pages + n_pages - 1 - j], 0, 0)
    grid_spec = pltpu.PrefetchScalarGridSpec(
        num_scalar_prefetch=1, grid=(nb, n_pages),
        in_specs=[pl.BlockSpec((1, HROWS, KVW), lambda b, j, pt: (b, 0, 0)),
                  pl.BlockSpec((1, PAGE, KVW), page),
                  pl.BlockSpec((1, PAGE, KVW), page)],
        out_specs=pl.BlockSpec((1, HROWS, HEAD_DIM), lambda b, j, pt: (b, 0, 0)),
        scratch_shapes=[pltpu.VMEM((HROWS, 1), F32), pltpu.VMEM((HROWS, KVW), F32)])
    return pl.pallas_call(
        _dec_sb_kernel, grid_spec=grid_spec,
        out_shape=jax.ShapeDtypeStruct((nb, HROWS, HEAD_DIM), BF16),
        compiler_params=_cparams(("arbitrary", "arbitrary")),
        name="dec_sb",
    )(page_ids, q_rows, cache_k, cache_v)


def _bf16_round(x):
    return x.astype(BF16).astype(F32)


def _dec_nsa_kernel(pt_ref, q_ref, gt_ref, kc_ref, vc_ref, ks_ref, vs_ref, wk_ref, wv_ref,
                    nks_ref, nvs_ref, nkw_ref, nvw_ref, o_ref,
                    pk_ref, m_ref, l_ref, acc_ref, oc_ref, *, past):
    del pt_ref
    j = pl.program_id(1)
    nj = pl.num_programs(1)
    q = q_ref[0]

    @pl.when(j == 0)
    def _():
        nc = kc_ref.shape[1]
        nb = nc // 2
        sc = _dot_nt(q, kc_ref[0].astype(BF16)) * SCALE
        pc = _softmax_masked(sc, _cmp_block_end(nc) <= past)
        oc_ref[...] = _group_lanes(_dot(pc.astype(BF16), vc_ref[0].astype(BF16)), NSA_R)
        pk = jnp.zeros((HROWS, nb), F32)
        tpos = jnp.full((1, 1), past, jnp.int32)
        for g in range(NSA_KV):
            pcs = jnp.sum(pc[g * NSA_R:(g + 1) * NSA_R], axis=0, keepdims=True)
            sel = _select_mask(pcs[:, :nb] + pcs[:, nb:], tpos)
            pk = pk + jnp.where(_rows_of_group(HROWS, g, NSA_R) & sel, 1.0, 0.0)
        pk_ref[...] = pk
        m_ref[...] = jnp.full(m_ref.shape, M_FLOOR, F32)
        l_ref[...] = jnp.zeros_like(l_ref)
        acc_ref[...] = jnp.zeros_like(acc_ref)

    nbl = pk_ref.shape[1]
    expand = (_iota((nbl, PAGE), 0) == (PAGE // SEL_BLOCK) * j + (_iota((nbl, PAGE), 1) >> 6)).astype(BF16)
    keep = _dot(pk_ref[...].astype(BF16), expand) > 0.5
    s = jnp.where(keep, _dot_nt(q, ks_ref[0].astype(BF16)) * SCALE, NEG)
    m_old = m_ref[...]
    m_new = jnp.maximum(m_old, jnp.max(s, axis=1, keepdims=True))
    p = jnp.exp(s - m_new)
    alpha = jnp.exp(m_old - m_new)
    l_ref[...] = alpha * l_ref[...] + jnp.sum(p, axis=1, keepdims=True)
    acc_ref[...] = alpha * acc_ref[...] + _dot(p.astype(BF16), vs_ref[0].astype(BF16))
    m_ref[...] = m_new

    @pl.when(j == nj - 1)
    def _():
        qf = q.astype(F32)
        s_new = jnp.sum(qf * _bf16_round(nks_ref[0]), axis=1, keepdims=True) * SCALE
        nblk = past // SEL_BLOCK
        s_new = jnp.where(pk_ref[:, nblk:nblk + 1] > 0.5, s_new, NEG)
        m_old = m_ref[...]
        m_new = jnp.maximum(m_old, s_new)
        p_new = jnp.exp(s_new - m_new)
        alpha = jnp.exp(m_old - m_new)
        l_fin = alpha * l_ref[...] + p_new
        acc = alpha * acc_ref[...] + _bf16_round(p_new) * _bf16_round(nvs_ref[0])
        o_s = _group_lanes(acc, NSA_R) * (1.0 / l_fin)

        wl = wk_ref.shape[1]
        sw = _dot_nt(q, wk_ref[0].astype(BF16)) * SCALE
        wpos = past - wl + _iota((1, wl), 1)
        wmask = wpos > past - WINDOW
        sw = jnp.where(wmask, sw, NEG)
        sw_new = jnp.sum(qf * _bf16_round(nkw_ref[0]), axis=1, keepdims=True) * SCALE
        mw = jnp.maximum(jnp.max(sw, axis=1, keepdims=True), sw_new)
        ew = jnp.where(wmask, jnp.exp(sw - mw), 0.0)
        ew_new = jnp.exp(sw_new - mw)
        lw = jnp.sum(ew, axis=1, keepdims=True) + ew_new
        accw = _dot(ew.astype(BF16), wv_ref[0].astype(BF16)) + _bf16_round(ew_new) * _bf16_round(nvw_ref[0])
        o_w = _group_lanes(accw, NSA_R) * (1.0 / lw)

        gt = gt_ref[0]
        o = gt[:, 0:1] * oc_ref[...] + gt[:, 1:2] * o_s + gt[:, 2:3] * o_w
        o_ref[0] = o.astype(o_ref.dtype)


def _dec_nsa(page_ids, q_rows, gates, k_cmp, v_cmp, cache_k, cache_v, win_k, win_v, new_ks, new_vs, new_kw,
             new_vw, n_pages):
    nb = q_rows.shape[0]
    nc = k_cmp.shape[1]
    wl = win_k.shape[1]
    past = n_pages * PAGE
    per_b = lambda shape: pl.BlockSpec((1,) + shape, lambda b, j, pt: (b, 0, 0))
    page = lambda b, j, pt: (pt[b * n_pages + j], 0, 0)
    grid_spec = pltpu.PrefetchScalarGridSpec(
        num_scalar_prefetch=1, grid=(nb, n_pages),
        in_specs=[per_b((HROWS, KVW)), per_b((HROWS, 3)), per_b((nc, KVW)), per_b((nc, KVW)),
                  pl.BlockSpec((1, PAGE, KVW), page), pl.BlockSpec((1, PAGE, KVW), page),
                  per_b((wl, KVW)), per_b((wl, KVW)),
                  per_b((1, KVW)), per_b((1, KVW)), per_b((1, KVW)), per_b((1, KVW))],
        out_specs=per_b((HROWS, HEAD_DIM)),
        scratch_shapes=[pltpu.VMEM((HROWS, nc // 2), F32), pltpu.VMEM((HROWS, 1), F32),
                        pltpu.VMEM((HROWS, 1), F32), pltpu.VMEM((HROWS, KVW), F32),
                        pltpu.VMEM((HROWS, HEAD_DIM), F32)])
    return pl.pallas_call(
        functools.partial(_dec_nsa_kernel, past=past), grid_spec=grid_spec,
        out_shape=jax.ShapeDtypeStruct((nb, HROWS, HEAD_DIM), BF16),
        compiler_params=_cparams(("arbitrary", "arbitrary")),
        name="dec_nsa",
    )(page_ids, q_rows, gates, k_cmp, v_cmp, cache_k, cache_v, win_k, win_v, new_ks, new_vs, new_kw, new_vw)


def _dec_mem_kernel(q_ref, k_ref, v_ref, o_ref):
    s = _dot_nt(q_ref[0], k_ref[0].astype(BF16)) * SCALE
    e = jnp.exp(s - jnp.max(s, axis=-1, keepdims=True))
    p = e * (1.0 / jnp.sum(e, axis=-1, keepdims=True))
    o_ref[0] = _group_lanes(_dot(p.astype(BF16), v_ref[0].astype(BF16)), 1).astype(o_ref.dtype)


def _dec_mem(q_rows, mem_k, mem_v):
    nb, n_mem = mem_k.shape[0], mem_k.shape[1]
    rows = q_rows.shape[1]
    per_b = lambda shape: pl.BlockSpec((1,) + shape, lambda b: (b, 0, 0))
    return pl.pallas_call(
        _dec_mem_kernel, grid=(nb,),
        in_specs=[per_b((rows, KVW)), per_b((n_mem, KVW)), per_b((n_mem, KVW))],
        out_specs=per_b((rows, HEAD_DIM)),
        out_shape=jax.ShapeDtypeStruct((nb, rows, HEAD_DIM), BF16),
        compiler_params=_cparams(("arbitrary",)),
        name="dec_mem",
    )(q_rows, mem_k, mem_v)


def _head_rows(q, n_heads, heads_per_group, rows):
    nb = q.shape[0]
    qh = q.reshape(nb, n_heads, 1, HEAD_DIM)
    onehot = (jnp.arange(n_heads)[:, None] // heads_per_group == jnp.arange(NSA_KV)[None, :])
    full = jnp.where(onehot[None, :, :, None], qh, 0.0).reshape(nb, n_heads, KVW)
    return jnp.pad(full, ((0, 0), (0, rows - n_heads), (0, 0))).astype(BF16)


def _even_odd(x, lanes_half):
    n = x.shape[-2]
    pad = [(0, 0)] * (x.ndim - 2) + [(0, lanes_half - n // 2), (0, 0)]
    return jnp.concatenate([jnp.pad(x[..., 0::2, :], pad), jnp.pad(x[..., 1::2, :], pad)], axis=-2)


def _rope_tables(pos):
    half = ROPE_DIMS // 2
    inv = jnp.power(ROPE_THETA, -jnp.arange(half, dtype=F32) * 2.0 / ROPE_DIMS)
    ang = pos.astype(F32)[:, None] * inv[None, :]
    cos, sin = jnp.cos(ang), jnp.sin(ang)
    n = pos.shape[0]
    rest = HEAD_DIM - ROPE_DIMS
    c = jnp.concatenate([cos, cos, jnp.ones((n, rest), F32)], axis=1)
    s_lo = jnp.concatenate([-sin, jnp.zeros((n, HEAD_DIM - half), F32)], axis=1)
    s_hi = jnp.concatenate([jnp.zeros((n, half), F32), sin, jnp.zeros((n, rest), F32)], axis=1)
    return c, s_lo, s_hi


C_Q, C_KC, C_VC, C_KS, C_VS, C_KW, C_VW = 0, 2048, 2560, 3072, 3584, 4096, 4608
C_SQ, C_SK, C_SV, C_MQ, C_BG, C_END = 5120, 6656, 7168, 7680, 8192, 20480
NG_COLS = NSA_HEADS * 3
NG_SRC = 5120


def kernel(x_prompt, mem_prompt, x_sample, cache_cmp_k, cache_cmp_v, cache_sel_k, cache_sel_v, cache_sb_k,
           cache_sb_v, state_win_k, state_win_v, cache_mem_k, cache_mem_v, page_table, norm1_w, w_in, g_nsa_q,
           g_cmp_k, g_sel_k, g_win_k, g_mem_q, g_mem_k, mem_norm_w, w_mem_kv, cmp_pos_k, cmp_k_w1, cmp_k_w2,
           cmp_pos_v, cmp_v_w1, cmp_v_w2, w_up_nsa, w_up_sb, w_up_mem, w_o, norm2_w, w_ff1, w_ff2):
    tp = x_prompt.shape[1]
    nb = x_sample.shape[0]
    n_pages = page_table.shape[1]
    past = n_pages * PAGE
    m = tp + nb
    tm = _row_tile(m, 832)
    assert x_prompt.shape[0] == 1 and x_sample.shape[1] == 1

    h = _rms_stack(x_prompt[0], x_sample[:, 0], norm1_w)
    w_main = jnp.concatenate([w_in[:, :NG_SRC], w_in[:, NG_SRC + NG_COLS:]], axis=1).astype(BF16)
    w_ng = jnp.pad(w_in[:, NG_SRC:NG_SRC + NG_COLS], ((0, 0), (0, HEAD_DIM - NG_COLS))).astype(BF16)
    pos = jnp.concatenate([jnp.arange(tp, dtype=jnp.int32), jnp.full((nb,), past, jnp.int32)])
    rope_c, rope_lo, rope_hi = _rope_tables(pos)
    modes = np.zeros((C_END // IN_TN,), np.int32)
    gidx = np.zeros((C_END // IN_TN,), np.int32)
    for c0, c1, md, gi in ((C_Q, C_KC, MODE_NORM_ROPE, 0), (C_KC, C_VC, MODE_NORM_ROPE, 1),
                           (C_KS, C_VS, MODE_NORM_ROPE, 2), (C_KW, C_VW, MODE_NORM_ROPE, 3),
                           (C_MQ, C_BG, MODE_NORM, 4), (C_BG, C_END, MODE_SIGMOID, 0)):
        modes[c0 // IN_TN:c1 // IN_TN] = md
        gidx[c0 // IN_TN:c1 // IN_TN] = gi
    gains = jnp.stack([g_nsa_q, g_cmp_k, g_sel_k, g_win_k, g_mem_q]).reshape(5, 1, HEAD_DIM)
    z = _inproj(h, w_main, jnp.asarray(modes), jnp.asarray(gidx), gains, rope_c, rope_lo, rope_hi, tm)
    ng = _mm(h, w_ng, tm=tm, tn=HEAD_DIM, tk=D_MODEL, out_dtype=F32, epi=_epi_sigmoid, name="head_gates")

    zp, zs = z[:tp], z[tp:]
    kv_out = lambda a, c0: a[:, c0:c0 + KVW].reshape(a.shape[0], NSA_KV, HEAD_DIM)
    p_kv = [kv_out(zp, c)[None] for c in (C_KC, C_VC, C_KS, C_VS, C_SK, C_SV)]
    s_kv = [kv_out(zs, c)[:, None] for c in (C_KC, C_VC, C_KS, C_VS, C_SK, C_SV)]
    keep = min(WINDOW, tp)
    p_win = [kv_out(zp[tp - keep:], c)[None] for c in (C_KW, C_VW)]
    new_win = [kv_out(zs, c)[:, None] for c in (C_KW, C_VW)]
    lb = state_win_k.shape[1]
    keep_s = min(WINDOW, past + 1)
    s_win = [jnp.concatenate([st, nw], axis=1)[:, lb + 1 - keep_s:]
             for st, nw in ((state_win_k, new_win[0]), (state_win_v, new_win[1]))]

    w1k, w2k = cmp_k_w1.astype(BF16), cmp_k_w2.astype(BF16)
    w1v, w2v = cmp_v_w1.astype(BF16), cmp_v_w2.astype(BF16)
    ids_p = jnp.arange(tp // PAGE, dtype=jnp.int32)
    n_cmp = tp // CMP_BLOCK
    pages_of = lambda c0: zp[:, c0:c0 + KVW].reshape(tp // PAGE, PAGE, KVW)
    k_cmp_p = _even_odd(_compress(pages_of(C_KC), ids_p, cmp_pos_k, w1k, w2k).reshape(n_cmp, KVW), HEAD_DIM)
    v_cmp_p = _even_odd(_compress(pages_of(C_VC), ids_p, cmp_pos_v, w1v, w2v).reshape(n_cmp, KVW), HEAD_DIM)

    gates_all = ng[:, :NG_COLS].reshape(m, NSA_KV, 3 * NSA_R)
    gates_t = jnp.transpose(gates_all[:tp], (1, 0, 2))
    kvb = zp[:, C_KC:C_SQ].astype(BF16)
    blk_of_key = jnp.arange(tp, dtype=jnp.int32) // SEL_BLOCK
    onehot = (blk_of_key[:, None] == jnp.arange(HEAD_DIM, dtype=jnp.int32)[None, :]).astype(BF16)
    ks_b = jnp.transpose(zp[:, C_KS:C_VS].astype(BF16).reshape(tp, NSA_KV, HEAD_DIM), (1, 0, 2))
    k_aug = jnp.concatenate([ks_b, jnp.broadcast_to(onehot[None], ks_b.shape)], axis=2)
    cb = lambda c0: (c0 - C_KC) // HEAD_DIM
    o_nsa_p = _nsa_prompt(z, gates_t, k_cmp_p, v_cmp_p, k_aug, kvb, tp, cb(C_VS), cb(C_KW), cb(C_VW))

    o_sb_p = _sb_prompt(zp[:, C_SQ:C_SK].astype(BF16), zp[:, C_SK:C_SV].astype(BF16),
                        zp[:, C_SV:C_MQ].astype(BF16))

    mem_h = _rms(mem_prompt[0], mem_norm_w, mem_prompt.shape[1])
    w_mkv = w_mem_kv.astype(BF16)
    n_mem = mem_prompt.shape[1]
    mw = MEM_HEADS * HEAD_DIM
    mk = _mm(mem_h, w_mkv, tm=n_mem, tn=mw, tk=D_MODEL, out_dtype=F32, epi=_epi_headnorm,
             extras=((g_mem_k.reshape(1, HEAD_DIM), 0),), ncols=mw, name="mem_k")
    mv = _mm(mem_h, w_mkv, tm=n_mem, tn=mw, tk=D_MODEL, out_dtype=F32, col0=1, name="mem_v")
    o_mem_p = _mem_prompt(z, C_MQ // mw, mk, mv, tp)

    pt = page_table.reshape(-1).astype(jnp.int32)
    paged = lambda c: c.reshape(c.shape[0], PAGE, KVW)
    k_cmp_s = _even_odd(_compress(paged(cache_cmp_k), pt, cmp_pos_k, w1k, w2k), HEAD_DIM)
    v_cmp_s = _even_odd(_compress(paged(cache_cmp_v), pt, cmp_pos_v, w1v, w2v), HEAD_DIM)
    q_nsa_s = _head_rows(zs[:, C_Q:C_KC], NSA_HEADS, NSA_R, HROWS)
    gates_s = ng[tp:, :NG_COLS].reshape(nb, NSA_HEADS, 3)
    flat = lambda a: a.reshape(a.shape[0], a.shape[1], KVW)
    new_row = lambda c0: zs[:, c0:c0 + KVW].reshape(nb, 1, KVW)
    o_nsa_s = _dec_nsa(pt, q_nsa_s, gates_s, k_cmp_s, v_cmp_s, paged(cache_sel_k), paged(cache_sel_v),
                       flat(state_win_k), flat(state_win_v), new_row(C_KS), new_row(C_VS), new_row(C_KW),
                       new_row(C_VW), n_pages)
    q_sb_s = _head_rows(zs[:, C_SQ:C_SK], SB_HEADS, SB_R, HROWS)
    o_sb_s = _dec_sb(pt, q_sb_s, paged(cache_sb_k), paged(cache_sb_v), n_pages)
    q_mem_s = _head_rows(zs[:, C_MQ:C_BG], MEM_HEADS, 1, 8)
    o_mem_s = _dec_mem(q_mem_s, flat(cache_mem_k), flat(cache_mem_v))

    o_nsa = jnp.concatenate([o_nsa_p, o_nsa_s.reshape(nb, NSA_HEADS * HEAD_DIM)], axis=0)
    o_sb = jnp.concatenate([o_sb_p, o_sb_s[:, :SB_HEADS].reshape(nb, SB_HEADS * HEAD_DIM)], axis=0)
    o_mem = jnp.concatenate([o_mem_p, o_mem_s[:, :MEM_HEADS].reshape(nb, mw)], axis=0)

    u = _merge_up(o_nsa, o_sb, o_mem, w_up_nsa.astype(BF16), w_up_sb.astype(BF16), w_up_mem.astype(BF16),
                  z, C_BG, tm, 512)
    x_all = jnp.concatenate([x_prompt[0], x_sample[:, 0]], axis=0)
    x1 = _mm(u, w_o.astype(BF16), tm=tm, tn=512, tk=D_MODEL, out_dtype=F32, epi=_epi_residual,
             extras=((x_all, 0),), name="out_proj")
    h2 = _rms(x1, norm2_w, _row_tile(m, 512))
    a = _mm(h2, w_ff1.astype(BF16), tm=tm, tn=1024, tk=D_MODEL, out_dtype=BF16, epi=_epi_relu2, name="ff1")
    y = _mm(a, w_ff2.astype(BF16), tm=tm, tn=512, tk=4096, out_dtype=F32, epi=_epi_residual,
            extras=((x1, 0),), name="ff2")

    y_prompt = y[:tp][None]
    y_sample = y[tp:][:, None]
    p_mem = [mk.reshape(1, n_mem, MEM_HEADS, HEAD_DIM), mv.reshape(1, n_mem, MEM_HEADS, HEAD_DIM)]
    return (y_prompt, y_sample, *p_kv, *p_win, *p_mem, *s_kv, *s_win)
```

```python
import functools
import math

import numpy as np
import jax
import jax.numpy as jnp
from jax import lax
from jax.experimental import pallas as pl
from jax.experimental.pallas import tpu as pltpu

F32 = jnp.float32
BF16 = jnp.bfloat16

D_MODEL = 4096
HEAD_DIM = 128
NSA_HEADS = 16
NSA_KV = 4
SB_HEADS = 12
MEM_HEADS = 4
CMP_BLOCK = 32
SEL_BLOCK = 64
N_SEL = 16
WINDOW = 512
PAGE = 128
ROPE_THETA = 500000.0
ROPE_DIMS = HEAD_DIM // 4
EPS = 1e-6
NEG = -1e30
M_FLOOR = -1e20
FORCE_BONUS = 1e4
SCALE = HEAD_DIM ** -0.5
KVW = NSA_KV * HEAD_DIM

VMEM_LIMIT = 56 * 1024 * 1024


def _row_tile(m, cap):
    for step in (16, 8):
        best = max((t for t in range(step, cap + 1, step) if m % t == 0), default=0)
        if best:
            return best
    raise ValueError(f"no row tile for {m}")


def _cparams(sem, vmem=VMEM_LIMIT):
    return pltpu.CompilerParams(dimension_semantics=sem, vmem_limit_bytes=vmem)


def _dot(a, b):
    return jnp.dot(a, b, preferred_element_type=F32)


def _dot_nt(a, b):
    return lax.dot_general(a, b, (((1,), (1,)), ((), ())), preferred_element_type=F32)


def _iota(shape, dim):
    return lax.broadcasted_iota(jnp.int32, shape, dim)


def _rms2_kernel(xp_ref, xs_ref, w_ref, o_ref, *, n_prompt_tiles):
    i = pl.program_id(0)
    x = jnp.where(i < n_prompt_tiles, xp_ref[...], xs_ref[...])
    ms = jnp.mean(x * x, axis=-1, keepdims=True)
    o_ref[...] = (x * lax.rsqrt(ms + EPS) * w_ref[...]).astype(o_ref.dtype)


def _rms_stack(xp, xs, w):
    tp, d = xp.shape
    ts = xs.shape[0]
    assert tp % ts == 0 and ts % 8 == 0
    npt = tp // ts
    return pl.pallas_call(
        functools.partial(_rms2_kernel, n_prompt_tiles=npt),
        grid=(npt + 1,),
        in_specs=[pl.BlockSpec((ts, d), lambda i: (jnp.minimum(i, npt - 1), 0)),
                  pl.BlockSpec((ts, d), lambda i: (0, 0)),
                  pl.BlockSpec((1, d), lambda i: (0, 0))],
        out_specs=pl.BlockSpec((ts, d), lambda i: (i, 0)),
        out_shape=jax.ShapeDtypeStruct((tp + ts, d), BF16),
        compiler_params=_cparams(("arbitrary",)),
        name="rms_stack",
    )(xp, xs, w.reshape(1, d))


def _rms_kernel(x_ref, w_ref, o_ref):
    x = x_ref[...]
    ms = jnp.mean(x * x, axis=-1, keepdims=True)
    o_ref[...] = (x * lax.rsqrt(ms + EPS) * w_ref[...]).astype(o_ref.dtype)


def _rms(x, w, tm):
    m, d = x.shape
    return pl.pallas_call(
        _rms_kernel, grid=(m // tm,),
        in_specs=[pl.BlockSpec((tm, d), lambda i: (i, 0)), pl.BlockSpec((1, d), lambda i: (0, 0))],
        out_specs=pl.BlockSpec((tm, d), lambda i: (i, 0)),
        out_shape=jax.ShapeDtypeStruct((m, d), BF16),
        compiler_params=_cparams(("arbitrary",)),
        name="rms",
    )(x, w.reshape(1, d))


MODE_PLAIN, MODE_NORM_ROPE, MODE_NORM, MODE_SIGMOID = 0, 1, 2, 3
IN_TN = 512


def _head_norm(z, g):
    outs = []
    for hh in range(z.shape[1] // HEAD_DIM):
        xs = z[:, hh * HEAD_DIM:(hh + 1) * HEAD_DIM]
        ms = jnp.mean(xs * xs, axis=-1, keepdims=True)
        outs.append(xs * lax.rsqrt(ms + EPS) * g)
    return outs


def _rope(x, c, s_lo, s_hi):
    half = ROPE_DIMS // 2
    return x * c + pltpu.roll(x, HEAD_DIM - half, 1) * s_lo + pltpu.roll(x, half, 1) * s_hi


def _inproj_kernel(mode_ref, gidx_ref, a_ref, b_ref, g_ref, c_ref, slo_ref, shi_ref, o_ref):
    j = pl.program_id(1)
    mode = mode_ref[j]
    o_ref[...] = _dot(a_ref[...], b_ref[...])

    @pl.when(mode == MODE_SIGMOID)
    def _():
        o_ref[...] = jax.nn.sigmoid(o_ref[...])

    @pl.when(mode == MODE_NORM)
    def _():
        o_ref[...] = jnp.concatenate(_head_norm(o_ref[...], g_ref[0]), axis=1)

    @pl.when(mode == MODE_NORM_ROPE)
    def _():
        c, slo, shi = c_ref[...], slo_ref[...], shi_ref[...]
        o_ref[...] = jnp.concatenate([_rope(y, c, slo, shi) for y in _head_norm(o_ref[...], g_ref[0])], axis=1)


def _inproj(h, w, modes, gidx, gains, rope_c, rope_lo, rope_hi, tm):
    m, k = h.shape
    n = w.shape[1]
    nj = n // IN_TN
    grid_spec = pltpu.PrefetchScalarGridSpec(
        num_scalar_prefetch=2, grid=(m // tm, nj),
        in_specs=[pl.BlockSpec((tm, k), lambda i, j, *_: (i, 0)),
                  pl.BlockSpec((k, IN_TN), lambda i, j, *_: (0, j)),
                  pl.BlockSpec((1, 1, HEAD_DIM), lambda i, j, mo, gi: (gi[j], 0, 0)),
                  pl.BlockSpec((tm, HEAD_DIM), lambda i, j, *_: (i, 0)),
                  pl.BlockSpec((tm, HEAD_DIM), lambda i, j, *_: (i, 0)),
                  pl.BlockSpec((tm, HEAD_DIM), lambda i, j, *_: (i, 0))],
        out_specs=pl.BlockSpec((tm, IN_TN), lambda i, j, *_: (i, j)))
    return pl.pallas_call(
        _inproj_kernel, grid_spec=grid_spec,
        out_shape=jax.ShapeDtypeStruct((m, n), F32),
        compiler_params=_cparams(("arbitrary", "arbitrary")),
        name="inproj",
    )(modes, gidx, h, w, gains, rope_c, rope_lo, rope_hi)


def _mm_kernel(*refs, nk, n_extra, epi):
    a_ref, b_ref = refs[0], refs[1]
    extra = refs[2:2 + n_extra]
    o_ref = refs[2 + n_extra]
    part = _dot(a_ref[...], b_ref[...])
    if nk == 1:
        o_ref[...] = epi(part, *[e[...] for e in extra]).astype(o_ref.dtype)
    else:
        acc_ref = refs[3 + n_extra]
        kk = pl.program_id(2)

        @pl.when(kk == 0)
        def _():
            acc_ref[...] = part

        @pl.when(kk > 0)
        def _():
            acc_ref[...] += part

        @pl.when(kk == nk - 1)
        def _():
            o_ref[...] = epi(acc_ref[...], *[e[...] for e in extra]).astype(o_ref.dtype)


def _mm(a, b, *, tm, tn, tk, out_dtype, epi=None, extras=(), col0=0, ncols=None, name="mm"):
    m, k = a.shape
    ncols = b.shape[1] - col0 * tn if ncols is None else ncols
    assert m % tm == 0 and ncols % tn == 0 and k % tk == 0
    nk = k // tk
    if epi is None:
        epi = lambda z: z
    in_specs = [pl.BlockSpec((tm, tk), lambda i, j, kk: (i, kk)),
                pl.BlockSpec((tk, tn), lambda i, j, kk: (kk, j + col0))]
    args = [a, b]
    for arr, off in extras:
        if arr.shape[0] == 1:
            in_specs.append(pl.BlockSpec((1, arr.shape[1]), lambda i, j, kk: (0, 0)))
        else:
            in_specs.append(pl.BlockSpec((tm, tn), lambda i, j, kk, off=off: (i, j + off)))
        args.append(arr)
    scratch = [pltpu.VMEM((tm, tn), F32)] if nk > 1 else []
    return pl.pallas_call(
        functools.partial(_mm_kernel, nk=nk, n_extra=len(extras), epi=epi),
        grid=(m // tm, ncols // tn, nk),
        in_specs=in_specs,
        out_specs=pl.BlockSpec((tm, tn), lambda i, j, kk: (i, j)),
        out_shape=jax.ShapeDtypeStruct((m, ncols), out_dtype),
        scratch_shapes=scratch,
        compiler_params=_cparams(("arbitrary", "arbitrary", "arbitrary")),
        name=name,
    )(*args)


def _epi_headnorm(z, g):
    return jnp.concatenate(_head_norm(z, g), axis=1)


def _epi_sigmoid(z):
    return jax.nn.sigmoid(z)


def _epi_residual(z, x):
    return x + z


def _epi_relu2(z):
    return jnp.square(jnp.maximum(z, 0.0))


def _up_kernel(on_ref, os_ref, om_ref, wn_ref, ws_ref, wm_ref, g0_ref, g1_ref, g2_ref, o_ref):
    u = (g0_ref[...] * _dot(on_ref[...], wn_ref[...])
         + g1_ref[...] * _dot(os_ref[...], ws_ref[...])
         + g2_ref[...] * _dot(om_ref[...], wm_ref[...]))
    o_ref[...] = u.astype(o_ref.dtype)


def _merge_up(o_nsa, o_sb, o_mem, w_n, w_s, w_m, z, gate_col0, tm, tn):
    m = o_nsa.shape[0]
    d = w_n.shape[1]
    gb = gate_col0 // tn
    nd = d // tn
    row = lambda kdim: pl.BlockSpec((tm, kdim), lambda i, j: (i, 0))
    col = lambda kdim: pl.BlockSpec((kdim, tn), lambda i, j: (0, j))
    gate = lambda c: pl.BlockSpec((tm, tn), lambda i, j, c=c: (i, gb + c * nd + j))
    return pl.pallas_call(
        _up_kernel, grid=(m // tm, nd),
        in_specs=[row(o_nsa.shape[1]), row(o_sb.shape[1]), row(o_mem.shape[1]),
                  col(w_n.shape[0]), col(w_s.shape[0]), col(w_m.shape[0]),
                  gate(0), gate(1), gate(2)],
        out_specs=pl.BlockSpec((tm, tn), lambda i, j: (i, j)),
        out_shape=jax.ShapeDtypeStruct((m, d), BF16),
        compiler_params=_cparams(("arbitrary", "arbitrary")),
        name="merge_up",
    )(o_nsa, o_sb, o_mem, w_n, w_s, w_m, z, z, z)


CMP_PAGES = 16
CMP_HID = 256


def _gelu_tanh(x):
    cdf = 0.5 * (1.0 + jnp.tanh(math.sqrt(2.0 / math.pi) * (x + 0.044715 * (x * x * x))))
    return x * cdf


def _compress_kernel(pt_ref, *refs):
    del pt_ref
    page_refs = refs[:CMP_PAGES]
    pos_ref, w1_ref, w2_ref, o_ref, buf_ref = refs[CMP_PAGES:]
    for c in range(CMP_PAGES):
        for g in range(NSA_KV):
            buf_ref[g, c * PAGE:(c + 1) * PAGE, :] = page_refs[c][pl.ds(g, PAGE, stride=NSA_KV), :]
    nrow = CMP_PAGES * (PAGE // CMP_BLOCK)
    acc = jnp.zeros((NSA_KV * nrow, CMP_HID), F32)
    def token(t):
        parts = [buf_ref[g, pl.ds(t, nrow, stride=CMP_BLOCK), :] for g in range(NSA_KV)]
        return (jnp.concatenate(parts, axis=0) + pos_ref[t:t + 1, :]).astype(BF16)

    accs = [acc, acc]
    for i, t in enumerate(range(0, CMP_BLOCK, 2)):
        x = jnp.concatenate([token(t), token(t + 1)], axis=1)
        accs[i % 2] = accs[i % 2] + _dot(x, w1_ref[t * HEAD_DIM:(t + 2) * HEAD_DIM, :])
    acc = accs[0] + accs[1]
    out = _dot(_gelu_tanh(acc).astype(BF16), w2_ref[...])
    for g in range(NSA_KV):
        o_ref[0, :, g * HEAD_DIM:(g + 1) * HEAD_DIM] = out[g * nrow:(g + 1) * nrow, :]


def _compress(pages, page_ids, pos, w1, w2):
    n_steps = page_ids.shape[0] // CMP_PAGES
    nrow = CMP_PAGES * (PAGE // CMP_BLOCK)
    page_specs = [pl.BlockSpec((PAGE * NSA_KV, HEAD_DIM), lambda s, pt, c=c: (pt[s * CMP_PAGES + c], 0))
                  for c in range(CMP_PAGES)]
    grid_spec = pltpu.PrefetchScalarGridSpec(
        num_scalar_prefetch=1, grid=(n_steps,),
        in_specs=page_specs + [pl.BlockSpec(pos.shape, lambda s, pt: (0, 0)),
                               pl.BlockSpec(w1.shape, lambda s, pt: (0, 0)),
                               pl.BlockSpec(w2.shape, lambda s, pt: (0, 0))],
        out_specs=pl.BlockSpec((1, nrow, KVW), lambda s, pt: (s, 0, 0)),
        scratch_shapes=[pltpu.VMEM((NSA_KV, CMP_PAGES * PAGE, HEAD_DIM), F32)])
    return pl.pallas_call(
        _compress_kernel, grid_spec=grid_spec,
        out_shape=jax.ShapeDtypeStruct((n_steps, nrow, KVW), F32),
        compiler_params=_cparams(("arbitrary",)),
        name="compress",
    )(page_ids, *([pages] * CMP_PAGES), pos, w1, w2)


def _softmax_masked(s, mask):
    sm = jnp.where(mask, s, NEG)
    e = jnp.exp(sm - jnp.max(sm, axis=-1, keepdims=True))
    p = e * (1.0 / jnp.sum(e, axis=-1, keepdims=True))
    return jnp.where(mask, p, 0.0)


def _cmp_block_end(n_lanes):
    half = n_lanes // 2
    lane = _iota((1, n_lanes), 1)
    blk = jnp.where(lane < half, 2 * lane, 2 * (lane - half) + 1)
    return (blk + 1) * CMP_BLOCK - 1


def _top_blocks(score, k_top):
    nb = score.shape[1]
    lane = _iota(score.shape, 1).astype(F32)
    work = score
    picked = jnp.zeros(score.shape, jnp.bool_)
    for _ in range(k_top):
        mx = jnp.max(work, axis=1, keepdims=True)
        first = jnp.min(jnp.where(work == mx, lane, float(nb)), axis=1, keepdims=True)
        hit = lane == first
        picked = picked | hit
        work = jnp.where(hit, -jnp.inf, work)
    return picked


def _select_mask(imp, tpos):
    nb = imp.shape[1]
    b = _iota((1, nb), 1)
    cur = tpos >> 6
    forced = (b == 0) | (b == cur) | (b == cur - 1)
    valid = (b * SEL_BLOCK) <= tpos
    score = jnp.where(valid, imp + FORCE_BONUS * forced.astype(F32), NEG)
    return _top_blocks(score, min(N_SEL, nb)) & valid


NSA_TQ = 128
NSA_TK = 512
NSA_TK_FULL = 1024
NSA_R = NSA_HEADS // NSA_KV
LOG2E = math.log2(math.e)


def _nsa_prompt_kernel(q_ref, gt_ref, kc_ref, vc_ref, ka_ref, vs_ref, kw_ref, vw_ref, o_ref,
                       m_ref, l_ref, acc_ref):
    i = pl.program_id(1)
    s0 = i * NSA_TQ
    rows = NSA_R * NSA_TQ
    q = q_ref[...]
    qr = jnp.concatenate([q[:, r * HEAD_DIM:(r + 1) * HEAD_DIM] for r in range(NSA_R)], axis=0).astype(BF16)
    tpos_q = s0 + _iota((NSA_TQ, 1), 0)
    tpos = jnp.concatenate([tpos_q] * NSA_R, axis=0)

    nc = kc_ref.shape[0]
    nb = nc // 2
    sc = _dot_nt(qr, kc_ref[...].astype(BF16)) * SCALE
    pc = _softmax_masked(sc, _cmp_block_end(nc) <= tpos)
    o_c = _dot(pc.astype(BF16), vc_ref[...].astype(BF16))
    pcs = pc[0:NSA_TQ]
    for r in range(1, NSA_R):
        pcs = pcs + pc[r * NSA_TQ:(r + 1) * NSA_TQ]
    imp = pcs[:, :nb] + pcs[:, nb:]

    wlen = WINDOW + NSA_TQ
    w0 = pl.multiple_of(jnp.maximum(s0 - WINDOW, 0), NSA_TQ)
    sw = _dot_nt(qr, kw_ref[pl.ds(w0, wlen), :]) * SCALE
    kpos = w0 + _iota((1, wlen), 1)
    pw = _softmax_masked(sw, (kpos <= tpos) & (kpos > tpos - WINDOW))
    o_w = _dot(pw.astype(BF16), vw_ref[pl.ds(w0, wlen), :])

    sel = _select_mask(imp, tpos_q)
    bias = jnp.where(sel, 0.0, NEG).astype(BF16)
    q_aug = jnp.concatenate([qr, jnp.concatenate([bias] * NSA_R, axis=0)], axis=1)
    m_ref[...] = jnp.full((rows, 1), M_FLOOR, F32)
    l_ref[...] = jnp.zeros((rows, 1), F32)
    acc_ref[...] = jnp.zeros((rows, HEAD_DIM), F32)

    def tile(k0, tk, masked):
        s = _dot_nt(q_aug, ka_ref[0, pl.ds(k0, tk), :])
        if masked:
            s = jnp.where(k0 + _iota((1, tk), 1) <= tpos, s, NEG)
        m_old = m_ref[...]
        m_new = jnp.maximum(m_old, jnp.max(s, axis=1, keepdims=True))
        p = jnp.exp2((s - m_new) * (SCALE * LOG2E))
        alpha = jnp.exp2((m_old - m_new) * (SCALE * LOG2E))
        l_ref[...] = alpha * l_ref[...] + jnp.sum(p, axis=1, keepdims=True)
        acc_ref[...] = alpha * acc_ref[...] + _dot(p.astype(BF16), vs_ref[pl.ds(k0, tk), :])
        m_ref[...] = m_new

    def full_body(kt, carry):
        tile(pl.multiple_of(kt * NSA_TK_FULL, NSA_TK_FULL), NSA_TK_FULL, False)
        return carry

    def edge_body(kt, carry):
        tile(pl.multiple_of(kt * NSA_TK, NSA_TK), NSA_TK, True)
        return carry

    n_full = s0 >> 10
    lax.fori_loop(0, n_full, full_body, 0)
    lax.fori_loop(n_full * (NSA_TK_FULL // NSA_TK), (s0 >> 9) + 1, edge_body, 0)
    o_s = acc_ref[...] * (1.0 / l_ref[...])

    gt = gt_ref[0]
    outs = []
    for r in range(NSA_R):
        rs = slice(r * NSA_TQ, (r + 1) * NSA_TQ)
        outs.append(gt[:, 3 * r:3 * r + 1] * o_c[rs] + gt[:, 3 * r + 1:3 * r + 2] * o_s[rs]
                    + gt[:, 3 * r + 2:3 * r + 3] * o_w[rs])
    o_ref[...] = jnp.concatenate(outs, axis=1).astype(o_ref.dtype)


def _nsa_prompt(z, gates_t, k_cmp, v_cmp, k_aug, kvb, t, vs_col, kw_col, vw_col):
    assert t % NSA_TK == 0 and t >= WINDOW + NSA_TQ
    nc = k_cmp.shape[0]
    rows = NSA_R * NSA_TQ
    return pl.pallas_call(
        _nsa_prompt_kernel, grid=(NSA_KV, t // NSA_TQ),
        in_specs=[pl.BlockSpec((NSA_TQ, NSA_R * HEAD_DIM), lambda g, i: (i, g)),
                  pl.BlockSpec((1, NSA_TQ, 3 * NSA_R), lambda g, i: (g, i, 0)),
                  pl.BlockSpec((nc, HEAD_DIM), lambda g, i: (0, g)),
                  pl.BlockSpec((nc, HEAD_DIM), lambda g, i: (0, g)),
                  pl.BlockSpec((1, t, 2 * HEAD_DIM), lambda g, i: (g, 0, 0)),
                  pl.BlockSpec((t, HEAD_DIM), lambda g, i: (0, vs_col + g)),
                  pl.BlockSpec((t, HEAD_DIM), lambda g, i: (0, kw_col + g)),
                  pl.BlockSpec((t, HEAD_DIM), lambda g, i: (0, vw_col + g))],
        out_specs=pl.BlockSpec((NSA_TQ, NSA_R * HEAD_DIM), lambda g, i: (i, g)),
        out_shape=jax.ShapeDtypeStruct((t, NSA_HEADS * HEAD_DIM), BF16),
        scratch_shapes=[pltpu.VMEM((rows, 1), F32), pltpu.VMEM((rows, 1), F32),
                        pltpu.VMEM((rows, HEAD_DIM), F32)],
        compiler_params=_cparams(("arbitrary", "arbitrary")),
        name="nsa_prompt",
    )(z, gates_t, k_cmp, v_cmp, k_aug, kvb, kvb, kvb)


SB_T = 256
SB_R = SB_HEADS // NSA_KV


def _strict_upper(n):
    return (_iota((n, n), 0) > _iota((n, n), 1)).astype(BF16)


def _sb_tile(z, causal, carry, u):
    lb = jnp.minimum(z, 0.0) - jnp.log1p(jnp.exp(-jnp.abs(z)))
    lk = lb - z
    if causal is not None:
        lk = jnp.where(causal, lk, 0.0)
    hi = lk.astype(BF16)
    lo = (lk - hi.astype(F32)).astype(BF16)
    after = _dot(hi, u) + _dot(lo, u) + carry
    a = jnp.exp(lb + after)
    if causal is not None:
        a = jnp.where(causal, a, 0.0)
    return a, carry + jnp.sum(lk, axis=1, keepdims=True)


def _sb_prompt_kernel(q_ref, k_ref, v_ref, o_ref, carry_ref, acc_ref):
    i = pl.program_id(1)
    rows = SB_R * SB_T
    q = q_ref[...]
    qr = jnp.concatenate([q[:, r * HEAD_DIM:(r + 1) * HEAD_DIM] for r in range(SB_R)], axis=0)
    u = _strict_upper(SB_T)
    s0 = pl.multiple_of(i * SB_T, SB_T)

    qi = jnp.concatenate([_iota((SB_T, 1), 0)] * SB_R, axis=0)
    causal = _iota((1, SB_T), 1) < qi
    z = _dot_nt(qr, k_ref[pl.ds(s0, SB_T), :]) * SCALE
    a, carry = _sb_tile(z, causal, jnp.zeros((rows, 1), F32), u)
    carry_ref[...] = carry
    acc_ref[...] = _dot(a.astype(BF16), v_ref[pl.ds(s0, SB_T), :])

    def body(kt, c):
        k0 = pl.multiple_of((i - 1 - kt) * SB_T, SB_T)
        zt = _dot_nt(qr, k_ref[pl.ds(k0, SB_T), :]) * SCALE
        at, cn = _sb_tile(zt, None, carry_ref[...], u)
        carry_ref[...] = cn
        acc_ref[...] += _dot(at.astype(BF16), v_ref[pl.ds(k0, SB_T), :])
        return c

    lax.fori_loop(0, i, body, 0)
    acc = acc_ref[...]
    o_ref[...] = jnp.concatenate([acc[r * SB_T:(r + 1) * SB_T] for r in range(SB_R)], axis=1).astype(o_ref.dtype)


def _sb_prompt(sq, sk, sv):
    t = sq.shape[0]
    assert t % SB_T == 0
    rows = SB_R * SB_T
    return pl.pallas_call(
        _sb_prompt_kernel, grid=(NSA_KV, t // SB_T),
        in_specs=[pl.BlockSpec((SB_T, SB_R * HEAD_DIM), lambda g, i: (i, g)),
                  pl.BlockSpec((t, HEAD_DIM), lambda g, i: (0, g)),
                  pl.BlockSpec((t, HEAD_DIM), lambda g, i: (0, g))],
        out_specs=pl.BlockSpec((SB_T, SB_R * HEAD_DIM), lambda g, i: (i, g)),
        out_shape=jax.ShapeDtypeStruct(sq.shape, BF16),
        scratch_shapes=[pltpu.VMEM((rows, 1), F32), pltpu.VMEM((rows, HEAD_DIM), F32)],
        compiler_params=_cparams(("arbitrary", "arbitrary")),
        name="sb_prompt",
    )(sq, sk, sv)


MEM_TQ = 512


def _mem_prompt_kernel(q_ref, k_ref, v_ref, o_ref):
    outs = []
    for h in range(MEM_HEADS):
        hs = slice(h * HEAD_DIM, (h + 1) * HEAD_DIM)
        s = _dot_nt(q_ref[:, hs].astype(BF16), k_ref[:, hs].astype(BF16)) * SCALE
        e = jnp.exp(s - jnp.max(s, axis=-1, keepdims=True))
        p = e * (1.0 / jnp.sum(e, axis=-1, keepdims=True))
        outs.append(_dot(p.astype(BF16), v_ref[:, hs].astype(BF16)))
    o_ref[...] = jnp.concatenate(outs, axis=1).astype(o_ref.dtype)


def _mem_prompt(z, mq_col, mk, mv, t):
    n_mem = mk.shape[0]
    w = MEM_HEADS * HEAD_DIM
    return pl.pallas_call(
        _mem_prompt_kernel, grid=(t // MEM_TQ,),
        in_specs=[pl.BlockSpec((MEM_TQ, w), lambda i: (i, mq_col)),
                  pl.BlockSpec((n_mem, w), lambda i: (0, 0)),
                  pl.BlockSpec((n_mem, w), lambda i: (0, 0))],
        out_specs=pl.BlockSpec((MEM_TQ, w), lambda i: (i, 0)),
        out_shape=jax.ShapeDtypeStruct((t, w), BF16),
        compiler_params=_cparams(("arbitrary",)),
        name="mem_prompt",
    )(z, mk, mv)


HROWS = 16


def _rows_of_group(n_rows, g, heads_per_group):
    row = _iota((n_rows, 1), 0)
    return (row >= g * heads_per_group) & (row < (g + 1) * heads_per_group)


def _group_lanes(full, heads_per_group):
    out = jnp.zeros((full.shape[0], HEAD_DIM), F32)
    for g in range(NSA_KV):
        out = out + jnp.where(_rows_of_group(full.shape[0], g, heads_per_group),
                              full[:, g * HEAD_DIM:(g + 1) * HEAD_DIM], 0.0)
    return out


def _token_rows(ref, n_tok, lead=()):
    return jnp.concatenate([ref[(*lead, pl.ds(g, n_tok, stride=NSA_KV), slice(None))] for g in range(NSA_KV)],
                           axis=1).astype(BF16)


def _page_specs(n_pages):
    return [pl.BlockSpec((PAGE * NSA_KV, HEAD_DIM), lambda b, pt, c=c: (pt[b * n_pages + c], 0))
            for c in range(n_pages)]


def _dec_sb_kernel(pt_ref, q_ref, *refs, n_pages):
    del pt_ref
    k_refs, v_refs, o_ref = refs[:n_pages], refs[n_pages:2 * n_pages], refs[2 * n_pages]
    q = q_ref[0]
    u = _strict_upper(PAGE)
    z = jnp.concatenate([_dot_nt(q, _token_rows(r, PAGE)) for r in k_refs], axis=1) * SCALE
    lb = jnp.minimum(z, 0.0) - jnp.log1p(jnp.exp(-jnp.abs(z)))
    lk = lb - z
    hi = lk.astype(BF16)
    lo = (lk - hi.astype(F32)).astype(BF16)
    page = lambda x, c: x[:, c * PAGE:(c + 1) * PAGE]
    carry = jnp.zeros((HROWS, 1), F32)
    acc = jnp.zeros((HROWS, KVW), F32)
    for c in reversed(range(n_pages)):
        after = _dot(page(hi, c), u) + _dot(page(lo, c), u) + carry
        a = jnp.exp(page(lb, c) + after)
        acc = acc + _dot(a.astype(BF16), _token_rows(v_refs[c], PAGE))
        carry = carry + jnp.sum(page(lk, c), axis=1, keepdims=True)
    o_ref[0] = _group_lanes(acc, SB_R).astype(o_ref.dtype)


def _dec_sb(page_ids, q_rows, cache_k, cache_v, n_pages):
    nb = q_rows.shape[0]
    grid_spec = pltpu.PrefetchScalarGridSpec(
        num_scalar_prefetch=1, grid=(nb,),
        in_specs=[pl.BlockSpec((1, HROWS, KVW), lambda b, pt: (b, 0, 0))] + 2 * _page_specs(n_pages),
        out_specs=pl.BlockSpec((1, HROWS, HEAD_DIM), lambda b, pt: (b, 0, 0)))
    return pl.pallas_call(
        functools.partial(_dec_sb_kernel, n_pages=n_pages), grid_spec=grid_spec,
        out_shape=jax.ShapeDtypeStruct((nb, HROWS, HEAD_DIM), BF16),
        compiler_params=_cparams(("arbitrary",)),
        name="dec_sb",
    )(page_ids, q_rows, *([cache_k] * n_pages), *([cache_v] * n_pages))


def _bf16_round(x):
    return x.astype(BF16).astype(F32)


def _pad_halves(x, half_rows):
    n = x.shape[0] // 2
    zeros = jnp.zeros((half_rows - n, x.shape[1]), x.dtype)
    return jnp.concatenate([x[:n], zeros, x[n:], zeros], axis=0)


def _dec_nsa_kernel(pt_ref, q_ref, gt_ref, kc_ref, vc_ref, wk_ref, wv_ref, nks_ref, nvs_ref, nkw_ref, nvw_ref,
                    *refs, past, n_pages):
    del pt_ref
    ks_refs, vs_refs, o_ref = refs[:n_pages], refs[n_pages:2 * n_pages], refs[2 * n_pages]
    q = q_ref[0]
    qf = q.astype(F32)

    nb = HEAD_DIM
    kc = _pad_halves(kc_ref[0].astype(BF16), nb)
    vc = _pad_halves(vc_ref[0].astype(BF16), nb)
    sc = _dot_nt(q, kc) * SCALE
    pc = _softmax_masked(sc, _cmp_block_end(2 * nb) <= past)
    o_c = _group_lanes(_dot(pc.astype(BF16), vc), NSA_R)
    imp = jnp.zeros((8, nb), F32)
    for g in range(NSA_KV):
        pcs = jnp.sum(pc[g * NSA_R:(g + 1) * NSA_R], axis=0, keepdims=True)
        imp = imp + jnp.where(_iota((8, 1), 0) == g, pcs[:, :nb] + pcs[:, nb:], 0.0)
    sel = _select_mask(imp, jnp.full((1, 1), past, jnp.int32)).astype(F32)
    pk = jnp.zeros((HROWS, nb), F32)
    for g in range(NSA_KV):
        pk = pk + jnp.where(_rows_of_group(HROWS, g, NSA_R), sel[g:g + 1, :], 0.0)

    s = jnp.concatenate([_dot_nt(q, _token_rows(r, PAGE)) for r in ks_refs], axis=1) * SCALE
    expand = (_iota((nb, past), 0) == (_iota((nb, past), 1) >> 6)).astype(BF16)
    s = jnp.where(_dot(pk.astype(BF16), expand) > 0.5, s, NEG)
    s_new = jnp.sum(qf * _bf16_round(nks_ref[0]), axis=1, keepdims=True) * SCALE
    nblk = past // SEL_BLOCK
    s_new = jnp.where(pk[:, nblk:nblk + 1] > 0.5, s_new, NEG)
    ms = jnp.maximum(jnp.maximum(jnp.max(s, axis=1, keepdims=True), s_new), M_FLOOR)
    e = jnp.exp(s - ms)
    e_new = jnp.exp(s_new - ms)
    ls = jnp.sum(e, axis=1, keepdims=True) + e_new
    acc = _bf16_round(e_new) * _bf16_round(nvs_ref[0])
    for c in range(n_pages):
        acc = acc + _dot(e[:, c * PAGE:(c + 1) * PAGE].astype(BF16), _token_rows(vs_refs[c], PAGE))
    o_s = _group_lanes(acc, NSA_R) * (1.0 / ls)

    wl = wk_ref.shape[1] // NSA_KV
    sw = _dot_nt(q, _token_rows(wk_ref, wl, (0,))) * SCALE
    wmask = past - wl + _iota((1, wl), 1) > past - WINDOW
    sw = jnp.where(wmask, sw, NEG)
    sw_new = jnp.sum(qf * _bf16_round(nkw_ref[0]), axis=1, keepdims=True) * SCALE
    mw = jnp.maximum(jnp.max(sw, axis=1, keepdims=True), sw_new)
    ew = jnp.where(wmask, jnp.exp(sw - mw), 0.0)
    ew_new = jnp.exp(sw_new - mw)
    lw = jnp.sum(ew, axis=1, keepdims=True) + ew_new
    accw = _dot(ew.astype(BF16), _token_rows(wv_ref, wl, (0,))) + _bf16_round(ew_new) * _bf16_round(nvw_ref[0])
    o_w = _group_lanes(accw, NSA_R) * (1.0 / lw)

    gt = gt_ref[0]
    o_ref[0] = (gt[:, 0:1] * o_c + gt[:, 1:2] * o_s + gt[:, 2:3] * o_w).astype(o_ref.dtype)


def _dec_nsa(page_ids, q_rows, gates, k_cmp, v_cmp, cache_k, cache_v, win_k, win_v, new_ks, new_vs, new_kw,
             new_vw, n_pages):
    nb = q_rows.shape[0]
    nc = k_cmp.shape[1]
    past = n_pages * PAGE
    assert nc <= 2 * HEAD_DIM and past // SEL_BLOCK < HEAD_DIM
    per_b = lambda shape: pl.BlockSpec((1,) + shape, lambda b, pt: (b, 0, 0))
    grid_spec = pltpu.PrefetchScalarGridSpec(
        num_scalar_prefetch=1, grid=(nb,),
        in_specs=[per_b((HROWS, KVW)), per_b((HROWS, 3)), per_b((nc, KVW)), per_b((nc, KVW)),
                  per_b(win_k.shape[1:]), per_b(win_v.shape[1:]),
                  per_b((1, KVW)), per_b((1, KVW)), per_b((1, KVW)), per_b((1, KVW))] + 2 * _page_specs(n_pages),
        out_specs=per_b((HROWS, HEAD_DIM)))
    return pl.pallas_call(
        functools.partial(_dec_nsa_kernel, past=past, n_pages=n_pages), grid_spec=grid_spec,
        out_shape=jax.ShapeDtypeStruct((nb, HROWS, HEAD_DIM), BF16),
        compiler_params=_cparams(("arbitrary",)),
        name="dec_nsa",
    )(page_ids, q_rows, gates, k_cmp, v_cmp, win_k, win_v, new_ks, new_vs, new_kw, new_vw,
      *([cache_k] * n_pages), *([cache_v] * n_pages))


def _dec_mem_kernel(q_ref, k_ref, v_ref, o_ref):
    n_mem = k_ref.shape[1] // NSA_KV
    s = _dot_nt(q_ref[0], _token_rows(k_ref, n_mem, (0,))) * SCALE
    e = jnp.exp(s - jnp.max(s, axis=-1, keepdims=True))
    p = e * (1.0 / jnp.sum(e, axis=-1, keepdims=True))
    o_ref[0] = _group_lanes(_dot(p.astype(BF16), _token_rows(v_ref, n_mem, (0,))), 1).astype(o_ref.dtype)


def _dec_mem(q_rows, mem_k, mem_v):
    nb = mem_k.shape[0]
    rows = q_rows.shape[1]
    per_b = lambda shape: pl.BlockSpec((1,) + shape, lambda b: (b, 0, 0))
    return pl.pallas_call(
        _dec_mem_kernel, grid=(nb,),
        in_specs=[per_b((rows, KVW)), per_b(mem_k.shape[1:]), per_b(mem_v.shape[1:])],
        out_specs=per_b((rows, HEAD_DIM)),
        out_shape=jax.ShapeDtypeStruct((nb, rows, HEAD_DIM), BF16),
        compiler_params=_cparams(("arbitrary",)),
        name="dec_mem",
    )(q_rows, mem_k, mem_v)


def _head_rows(q, n_heads, heads_per_group, rows):
    nb = q.shape[0]
    qh = q.reshape(nb, n_heads, 1, HEAD_DIM)
    onehot = (jnp.arange(n_heads)[:, None] // heads_per_group == jnp.arange(NSA_KV)[None, :])
    full = jnp.where(onehot[None, :, :, None], qh, 0.0).reshape(nb, n_heads, KVW)
    return jnp.pad(full, ((0, 0), (0, rows - n_heads), (0, 0))).astype(BF16)


def _even_odd(x, lanes_half):
    n = x.shape[-2]
    pad = [(0, 0)] * (x.ndim - 2) + [(0, lanes_half - n // 2), (0, 0)]
    return jnp.concatenate([jnp.pad(x[..., 0::2, :], pad), jnp.pad(x[..., 1::2, :], pad)], axis=-2)


def _rope_tables(pos):
    half = ROPE_DIMS // 2
    inv = jnp.power(ROPE_THETA, -jnp.arange(half, dtype=F32) * 2.0 / ROPE_DIMS)
    ang = pos.astype(F32)[:, None] * inv[None, :]
    cos, sin = jnp.cos(ang), jnp.sin(ang)
    n = pos.shape[0]
    rest = HEAD_DIM - ROPE_DIMS
    c = jnp.concatenate([cos, cos, jnp.ones((n, rest), F32)], axis=1)
    s_lo = jnp.concatenate([-sin, jnp.zeros((n, HEAD_DIM - half), F32)], axis=1)
    s_hi = jnp.concatenate([jnp.zeros((n, half), F32), sin, jnp.zeros((n, rest), F32)], axis=1)
    return c, s_lo, s_hi


C_Q, C_KC, C_VC, C_KS, C_VS, C_KW, C_VW = 0, 2048, 2560, 3072, 3584, 4096, 4608
C_SQ, C_SK, C_SV, C_MQ, C_BG, C_END = 5120, 6656, 7168, 7680, 8192, 20480
NG_COLS = NSA_HEADS * 3
NG_SRC = 5120


def kernel(x_prompt, mem_prompt, x_sample, cache_cmp_k, cache_cmp_v, cache_sel_k, cache_sel_v, cache_sb_k,
           cache_sb_v, state_win_k, state_win_v, cache_mem_k, cache_mem_v, page_table, norm1_w, w_in, g_nsa_q,
           g_cmp_k, g_sel_k, g_win_k, g_mem_q, g_mem_k, mem_norm_w, w_mem_kv, cmp_pos_k, cmp_k_w1, cmp_k_w2,
           cmp_pos_v, cmp_v_w1, cmp_v_w2, w_up_nsa, w_up_sb, w_up_mem, w_o, norm2_w, w_ff1, w_ff2):
    tp = x_prompt.shape[1]
    nb = x_sample.shape[0]
    n_pages = page_table.shape[1]
    past = n_pages * PAGE
    m = tp + nb
    tm = _row_tile(m, 832)
    assert x_prompt.shape[0] == 1 and x_sample.shape[1] == 1

    h = _rms_stack(x_prompt[0], x_sample[:, 0], norm1_w)
    w_main = jnp.concatenate([w_in[:, :NG_SRC], w_in[:, NG_SRC + NG_COLS:]], axis=1).astype(BF16)
    w_ng = jnp.pad(w_in[:, NG_SRC:NG_SRC + NG_COLS], ((0, 0), (0, HEAD_DIM - NG_COLS))).astype(BF16)
    pos = jnp.concatenate([jnp.arange(tp, dtype=jnp.int32), jnp.full((nb,), past, jnp.int32)])
    rope_c, rope_lo, rope_hi = _rope_tables(pos)
    modes = np.zeros((C_END // IN_TN,), np.int32)
    gidx = np.zeros((C_END // IN_TN,), np.int32)
    for c0, c1, md, gi in ((C_Q, C_KC, MODE_NORM_ROPE, 0), (C_KC, C_VC, MODE_NORM_ROPE, 1),
                           (C_KS, C_VS, MODE_NORM_ROPE, 2), (C_KW, C_VW, MODE_NORM_ROPE, 3),
                           (C_MQ, C_BG, MODE_NORM, 4), (C_BG, C_END, MODE_SIGMOID, 0)):
        modes[c0 // IN_TN:c1 // IN_TN] = md
        gidx[c0 // IN_TN:c1 // IN_TN] = gi
    gains = jnp.stack([g_nsa_q, g_cmp_k, g_sel_k, g_win_k, g_mem_q]).reshape(5, 1, HEAD_DIM)
    z = _inproj(h, w_main, jnp.asarray(modes), jnp.asarray(gidx), gains, rope_c, rope_lo, rope_hi, tm)
    ng = _mm(h, w_ng, tm=tm, tn=HEAD_DIM, tk=D_MODEL, out_dtype=F32, epi=_epi_sigmoid, name="head_gates")

    zp, zs = z[:tp], z[tp:]
    kv_out = lambda a, c0: a[:, c0:c0 + KVW].reshape(a.shape[0], NSA_KV, HEAD_DIM)
    p_kv = [kv_out(zp, c)[None] for c in (C_KC, C_VC, C_KS, C_VS, C_SK, C_SV)]
    s_kv = [kv_out(zs, c)[:, None] for c in (C_KC, C_VC, C_KS, C_VS, C_SK, C_SV)]
    keep = min(WINDOW, tp)
    p_win = [kv_out(zp[tp - keep:], c)[None] for c in (C_KW, C_VW)]
    new_win = [kv_out(zs, c)[:, None] for c in (C_KW, C_VW)]
    lb = state_win_k.shape[1]
    keep_s = min(WINDOW, past + 1)
    s_win = [jnp.concatenate([st, nw], axis=1)[:, lb + 1 - keep_s:]
             for st, nw in ((state_win_k, new_win[0]), (state_win_v, new_win[1]))]

    w1k, w2k = cmp_k_w1.astype(BF16), cmp_k_w2.astype(BF16)
    w1v, w2v = cmp_v_w1.astype(BF16), cmp_v_w2.astype(BF16)
    ids_p = jnp.arange(tp // PAGE, dtype=jnp.int32)
    n_cmp = tp // CMP_BLOCK
    token_rows = lambda a: a.reshape(-1, HEAD_DIM)
    k_cmp_p = _even_odd(_compress(token_rows(p_kv[0]), ids_p, cmp_pos_k, w1k, w2k).reshape(n_cmp, KVW), HEAD_DIM)
    v_cmp_p = _even_odd(_compress(token_rows(p_kv[1]), ids_p, cmp_pos_v, w1v, w2v).reshape(n_cmp, KVW), HEAD_DIM)

    gates_all = ng[:, :NG_COLS].reshape(m, NSA_KV, 3 * NSA_R)
    gates_t = jnp.transpose(gates_all[:tp], (1, 0, 2))
    kvb = zp[:, C_KC:C_SQ].astype(BF16)
    blk_of_key = jnp.arange(tp, dtype=jnp.int32) // SEL_BLOCK
    onehot = (blk_of_key[:, None] == jnp.arange(HEAD_DIM, dtype=jnp.int32)[None, :]).astype(BF16)
    ks_b = jnp.transpose(zp[:, C_KS:C_VS].astype(BF16).reshape(tp, NSA_KV, HEAD_DIM), (1, 0, 2))
    k_aug = jnp.concatenate([ks_b, jnp.broadcast_to(onehot[None], ks_b.shape)], axis=2)
    cb = lambda c0: (c0 - C_KC) // HEAD_DIM
    o_nsa_p = _nsa_prompt(z, gates_t, k_cmp_p, v_cmp_p, k_aug, kvb, tp, cb(C_VS), cb(C_KW), cb(C_VW))

    o_sb_p = _sb_prompt(zp[:, C_SQ:C_SK].astype(BF16), zp[:, C_SK:C_SV].astype(BF16),
                        zp[:, C_SV:C_MQ].astype(BF16))

    mem_h = _rms(mem_prompt[0], mem_norm_w, mem_prompt.shape[1])
    w_mkv = w_mem_kv.astype(BF16)
    n_mem = mem_prompt.shape[1]
    mw = MEM_HEADS * HEAD_DIM
    mk = _mm(mem_h, w_mkv, tm=n_mem, tn=mw, tk=D_MODEL, out_dtype=F32, epi=_epi_headnorm,
             extras=((g_mem_k.reshape(1, HEAD_DIM), 0),), ncols=mw, name="mem_k")
    mv = _mm(mem_h, w_mkv, tm=n_mem, tn=mw, tk=D_MODEL, out_dtype=F32, col0=1, name="mem_v")
    o_mem_p = _mem_prompt(z, C_MQ // mw, mk, mv, tp)

    pt = page_table.reshape(-1).astype(jnp.int32)
    n_cmp_s = past // CMP_BLOCK
    k_cmp_s = _even_odd(_compress(token_rows(cache_cmp_k), pt, cmp_pos_k, w1k, w2k), n_cmp_s // 2)
    v_cmp_s = _even_odd(_compress(token_rows(cache_cmp_v), pt, cmp_pos_v, w1v, w2v), n_cmp_s // 2)
    q_nsa_s = _head_rows(zs[:, C_Q:C_KC], NSA_HEADS, NSA_R, HROWS)
    gates_s = ng[tp:, :NG_COLS].reshape(nb, NSA_HEADS, 3)
    seq_rows = lambda a: a.reshape(a.shape[0], -1, HEAD_DIM)
    new_row = lambda c0: zs[:, c0:c0 + KVW].reshape(nb, 1, KVW)
    o_nsa_s = _dec_nsa(pt, q_nsa_s, gates_s, k_cmp_s, v_cmp_s, token_rows(cache_sel_k), token_rows(cache_sel_v),
                       seq_rows(state_win_k), seq_rows(state_win_v), new_row(C_KS), new_row(C_VS), new_row(C_KW),
                       new_row(C_VW), n_pages)
    q_sb_s = _head_rows(zs[:, C_SQ:C_SK], SB_HEADS, SB_R, HROWS)
    o_sb_s = _dec_sb(pt, q_sb_s, token_rows(cache_sb_k), token_rows(cache_sb_v), n_pages)
    q_mem_s = _head_rows(zs[:, C_MQ:C_BG], MEM_HEADS, 1, 8)
    o_mem_s = _dec_mem(q_mem_s, seq_rows(cache_mem_k), seq_rows(cache_mem_v))

    o_nsa = jnp.concatenate([o_nsa_p, o_nsa_s.reshape(nb, NSA_HEADS * HEAD_DIM)], axis=0)
    o_sb = jnp.concatenate([o_sb_p, o_sb_s[:, :SB_HEADS].reshape(nb, SB_HEADS * HEAD_DIM)], axis=0)
    o_mem = jnp.concatenate([o_mem_p, o_mem_s[:, :MEM_HEADS].reshape(nb, mw)], axis=0)

    u = _merge_up(o_nsa, o_sb, o_mem, w_up_nsa.astype(BF16), w_up_sb.astype(BF16), w_up_mem.astype(BF16),
                  z, C_BG, tm, 512)
    x_all = jnp.concatenate([x_prompt[0], x_sample[:, 0]], axis=0)
    x1 = _mm(u, w_o.astype(BF16), tm=tm, tn=512, tk=D_MODEL, out_dtype=F32, epi=_epi_residual,
             extras=((x_all, 0),), name="out_proj")
    h2 = _rms(x1, norm2_w, _row_tile(m, 512))
    a = _mm(h2, w_ff1.astype(BF16), tm=tm, tn=1024, tk=D_MODEL, out_dtype=BF16, epi=_epi_relu2, name="ff1")
    y = _mm(a, w_ff2.astype(BF16), tm=tm, tn=512, tk=4096, out_dtype=F32, epi=_epi_residual,
            extras=((x1, 0),), name="ff2")

    y_prompt = y[:tp][None]
    y_sample = y[tp:][:, None]
    p_mem = [mk.reshape(1, n_mem, MEM_HEADS, HEAD_DIM), mv.reshape(1, n_mem, MEM_HEADS, HEAD_DIM)]
    return (y_prompt, y_sample, *p_kv, *p_win, *p_mem, *s_kv, *s_win)
```

```python
import functools
import math

import numpy as np
import jax
import jax.numpy as jnp
from jax import lax
from jax.experimental import pallas as pl
from jax.experimental.pallas import tpu as pltpu

F32 = jnp.float32
BF16 = jnp.bfloat16

D_MODEL = 4096
HEAD_DIM = 128
NSA_HEADS = 16
NSA_KV = 4
SB_HEADS = 12
MEM_HEADS = 4
CMP_BLOCK = 32
SEL_BLOCK = 64
N_SEL = 16
WINDOW = 512
PAGE = 128
ROPE_THETA = 500000.0
ROPE_DIMS = HEAD_DIM // 4
EPS = 1e-6
NEG = -1e30
M_FLOOR = -1e20
FORCE_BONUS = 1e4
SCALE = HEAD_DIM ** -0.5
KVW = NSA_KV * HEAD_DIM

VMEM_LIMIT = 56 * 1024 * 1024


def _row_tile(m, cap):
    for step in (16, 8):
        best = max((t for t in range(step, cap + 1, step) if m % t == 0), default=0)
        if best:
            return best
    raise ValueError(f"no row tile for {m}")


def _cparams(sem, vmem=VMEM_LIMIT):
    return pltpu.CompilerParams(dimension_semantics=sem, vmem_limit_bytes=vmem)


def _dot(a, b):
    return jnp.dot(a, b, preferred_element_type=F32)


def _dot_nt(a, b):
    return lax.dot_general(a, b, (((1,), (1,)), ((), ())), preferred_element_type=F32)


def _iota(shape, dim):
    return lax.broadcasted_iota(jnp.int32, shape, dim)


def _rms2_kernel(xp_ref, xs_ref, w_ref, o_ref, *, n_prompt_tiles):
    i = pl.program_id(0)
    x = jnp.where(i < n_prompt_tiles, xp_ref[...], xs_ref[...])
    ms = jnp.mean(x * x, axis=-1, keepdims=True)
    o_ref[...] = (x * lax.rsqrt(ms + EPS) * w_ref[...]).astype(o_ref.dtype)


def _rms_stack(xp, xs, w):
    tp, d = xp.shape
    ts = xs.shape[0]
    assert tp % ts == 0 and ts % 8 == 0
    npt = tp // ts
    return pl.pallas_call(
        functools.partial(_rms2_kernel, n_prompt_tiles=npt),
        grid=(npt + 1,),
        in_specs=[pl.BlockSpec((ts, d), lambda i: (jnp.minimum(i, npt - 1), 0)),
                  pl.BlockSpec((ts, d), lambda i: (0, 0)),
                  pl.BlockSpec((1, d), lambda i: (0, 0))],
        out_specs=pl.BlockSpec((ts, d), lambda i: (i, 0)),
        out_shape=jax.ShapeDtypeStruct((tp + ts, d), BF16),
        compiler_params=_cparams(("arbitrary",)),
        name="rms_stack",
    )(xp, xs, w.reshape(1, d))


def _rms_kernel(x_ref, w_ref, o_ref):
    x = x_ref[...]
    ms = jnp.mean(x * x, axis=-1, keepdims=True)
    o_ref[...] = (x * lax.rsqrt(ms + EPS) * w_ref[...]).astype(o_ref.dtype)


def _rms(x, w, tm):
    m, d = x.shape
    return pl.pallas_call(
        _rms_kernel, grid=(m // tm,),
        in_specs=[pl.BlockSpec((tm, d), lambda i: (i, 0)), pl.BlockSpec((1, d), lambda i: (0, 0))],
        out_specs=pl.BlockSpec((tm, d), lambda i: (i, 0)),
        out_shape=jax.ShapeDtypeStruct((m, d), BF16),
        compiler_params=_cparams(("arbitrary",)),
        name="rms",
    )(x, w.reshape(1, d))


MODE_PLAIN, MODE_NORM_ROPE, MODE_NORM, MODE_SIGMOID = 0, 1, 2, 3
IN_TN = 512
IN_PARTS = 2


def _head_norm(z, g):
    outs = []
    for hh in range(z.shape[1] // HEAD_DIM):
        xs = z[:, hh * HEAD_DIM:(hh + 1) * HEAD_DIM]
        ms = jnp.mean(xs * xs, axis=-1, keepdims=True)
        outs.append(xs * lax.rsqrt(ms + EPS) * g)
    return outs


def _rope(x, c, s_lo, s_hi):
    half = ROPE_DIMS // 2
    return x * c + pltpu.roll(x, HEAD_DIM - half, 1) * s_lo + pltpu.roll(x, half, 1) * s_hi


def _inproj_kernel(mode_ref, gidx_ref, a_ref, b_ref, g_ref, c_ref, slo_ref, shi_ref, o_ref):
    j = pl.program_id(1)
    o_ref[...] = _dot(a_ref[...], b_ref[...])
    for part in range(IN_PARTS):
        cols = slice(part * IN_TN, (part + 1) * IN_TN)
        mode = mode_ref[j * IN_PARTS + part]
        gain = g_ref[gidx_ref[j * IN_PARTS + part]]

        @pl.when(mode == MODE_SIGMOID)
        def _():
            o_ref[:, cols] = jax.nn.sigmoid(o_ref[:, cols])

        @pl.when(mode == MODE_NORM)
        def _():
            o_ref[:, cols] = jnp.concatenate(_head_norm(o_ref[:, cols], gain), axis=1)

        @pl.when(mode == MODE_NORM_ROPE)
        def _():
            c, slo, shi = c_ref[...], slo_ref[...], shi_ref[...]
            o_ref[:, cols] = jnp.concatenate([_rope(y, c, slo, shi) for y in _head_norm(o_ref[:, cols], gain)],
                                             axis=1)


def _inproj(h, w, modes, gidx, gains, rope_c, rope_lo, rope_hi, tm):
    m, k = h.shape
    n = w.shape[1]
    tn = IN_PARTS * IN_TN
    grid_spec = pltpu.PrefetchScalarGridSpec(
        num_scalar_prefetch=2, grid=(m // tm, n // tn),
        in_specs=[pl.BlockSpec((tm, k), lambda i, j, *_: (i, 0)),
                  pl.BlockSpec((k, tn), lambda i, j, *_: (0, j)),
                  pl.BlockSpec(gains.shape, lambda i, j, *_: (0, 0, 0)),
                  pl.BlockSpec((tm, HEAD_DIM), lambda i, j, *_: (i, 0)),
                  pl.BlockSpec((tm, HEAD_DIM), lambda i, j, *_: (i, 0)),
                  pl.BlockSpec((tm, HEAD_DIM), lambda i, j, *_: (i, 0))],
        out_specs=pl.BlockSpec((tm, tn), lambda i, j, *_: (i, j)))
    return pl.pallas_call(
        _inproj_kernel, grid_spec=grid_spec,
        out_shape=jax.ShapeDtypeStruct((m, n), F32),
        compiler_params=_cparams(("arbitrary", "arbitrary")),
        name="inproj",
    )(modes, gidx, h, w, gains, rope_c, rope_lo, rope_hi)


def _mm_kernel(*refs, nk, n_extra, epi):
    a_ref, b_ref = refs[0], refs[1]
    extra = refs[2:2 + n_extra]
    o_ref = refs[2 + n_extra]
    part = _dot(a_ref[...], b_ref[...])
    if nk == 1:
        o_ref[...] = epi(part, *[e[...] for e in extra]).astype(o_ref.dtype)
    else:
        acc_ref = refs[3 + n_extra]
        kk = pl.program_id(2)

        @pl.when(kk == 0)
        def _():
            acc_ref[...] = part

        @pl.when(kk > 0)
        def _():
            acc_ref[...] += part

        @pl.when(kk == nk - 1)
        def _():
            o_ref[...] = epi(acc_ref[...], *[e[...] for e in extra]).astype(o_ref.dtype)


def _mm(a, b, *, tm, tn, tk, out_dtype, epi=None, extras=(), col0=0, ncols=None, name="mm"):
    m, k = a.shape
    ncols = b.shape[1] - col0 * tn if ncols is None else ncols
    assert m % tm == 0 and ncols % tn == 0 and k % tk == 0
    nk = k // tk
    if epi is None:
        epi = lambda z: z
    in_specs = [pl.BlockSpec((tm, tk), lambda i, j, kk: (i, kk)),
                pl.BlockSpec((tk, tn), lambda i, j, kk: (kk, j + col0))]
    args = [a, b]
    for arr, off in extras:
        if arr.shape[0] == 1:
            in_specs.append(pl.BlockSpec((1, arr.shape[1]), lambda i, j, kk: (0, 0)))
        else:
            in_specs.append(pl.BlockSpec((tm, tn), lambda i, j, kk, off=off: (i, j + off)))
        args.append(arr)
    scratch = [pltpu.VMEM((tm, tn), F32)] if nk > 1 else []
    return pl.pallas_call(
        functools.partial(_mm_kernel, nk=nk, n_extra=len(extras), epi=epi),
        grid=(m // tm, ncols // tn, nk),
        in_specs=in_specs,
        out_specs=pl.BlockSpec((tm, tn), lambda i, j, kk: (i, j)),
        out_shape=jax.ShapeDtypeStruct((m, ncols), out_dtype),
        scratch_shapes=scratch,
        compiler_params=_cparams(("arbitrary", "arbitrary", "arbitrary")),
        name=name,
    )(*args)


def _epi_headnorm(z, g):
    return jnp.concatenate(_head_norm(z, g), axis=1)


def _epi_sigmoid(z):
    return jax.nn.sigmoid(z)


def _epi_residual(z, x):
    return x + z


def _epi_relu2(z):
    return jnp.square(jnp.maximum(z, 0.0))


def _up_kernel(on_ref, os_ref, om_ref, wn_ref, ws_ref, wm_ref, g0_ref, g1_ref, g2_ref, o_ref):
    u = (g0_ref[...] * _dot(on_ref[...], wn_ref[...])
         + g1_ref[...] * _dot(os_ref[...], ws_ref[...])
         + g2_ref[...] * _dot(om_ref[...], wm_ref[...]))
    o_ref[...] = u.astype(o_ref.dtype)


def _merge_up(o_nsa, o_sb, o_mem, w_n, w_s, w_m, z, gate_col0, tm, tn):
    m = o_nsa.shape[0]
    d = w_n.shape[1]
    gb = gate_col0 // tn
    nd = d // tn
    row = lambda kdim: pl.BlockSpec((tm, kdim), lambda i, j: (i, 0))
    col = lambda kdim: pl.BlockSpec((kdim, tn), lambda i, j: (0, j))
    gate = lambda c: pl.BlockSpec((tm, tn), lambda i, j, c=c: (i, gb + c * nd + j))
    return pl.pallas_call(
        _up_kernel, grid=(m // tm, nd),
        in_specs=[row(o_nsa.shape[1]), row(o_sb.shape[1]), row(o_mem.shape[1]),
                  col(w_n.shape[0]), col(w_s.shape[0]), col(w_m.shape[0]),
                  gate(0), gate(1), gate(2)],
        out_specs=pl.BlockSpec((tm, tn), lambda i, j: (i, j)),
        out_shape=jax.ShapeDtypeStruct((m, d), BF16),
        compiler_params=_cparams(("arbitrary", "arbitrary")),
        name="merge_up",
    )(o_nsa, o_sb, o_mem, w_n, w_s, w_m, z, z, z)


CMP_PAGES = 16
CMP_HID = 256


def _gelu_tanh(x):
    cdf = 0.5 * (1.0 + jnp.tanh(math.sqrt(2.0 / math.pi) * (x + 0.044715 * (x * x * x))))
    return x * cdf


def _compress_kernel(pt_ref, *refs):
    del pt_ref
    page_refs = refs[:CMP_PAGES]
    pos_ref, w1_ref, w2_ref, o_ref, buf_ref = refs[CMP_PAGES:]
    for c in range(CMP_PAGES):
        for g in range(NSA_KV):
            buf_ref[g, c * PAGE:(c + 1) * PAGE, :] = page_refs[c][pl.ds(g, PAGE, stride=NSA_KV), :]
    nrow = CMP_PAGES * (PAGE // CMP_BLOCK)
    acc = jnp.zeros((NSA_KV * nrow, CMP_HID), F32)
    def token(t):
        parts = [buf_ref[g, pl.ds(t, nrow, stride=CMP_BLOCK), :] for g in range(NSA_KV)]
        return (jnp.concatenate(parts, axis=0) + pos_ref[t:t + 1, :]).astype(BF16)

    accs = [acc, acc]
    for i, t in enumerate(range(0, CMP_BLOCK, 2)):
        x = jnp.concatenate([token(t), token(t + 1)], axis=1)
        accs[i % 2] = accs[i % 2] + _dot(x, w1_ref[t * HEAD_DIM:(t + 2) * HEAD_DIM, :])
    acc = accs[0] + accs[1]
    out = _dot(_gelu_tanh(acc).astype(BF16), w2_ref[...])
    for g in range(NSA_KV):
        o_ref[0, :, g * HEAD_DIM:(g + 1) * HEAD_DIM] = out[g * nrow:(g + 1) * nrow, :]


def _compress(pages, page_ids, pos, w1, w2):
    n_steps = page_ids.shape[0] // CMP_PAGES
    nrow = CMP_PAGES * (PAGE // CMP_BLOCK)
    page_specs = [pl.BlockSpec((PAGE * NSA_KV, HEAD_DIM), lambda s, pt, c=c: (pt[s * CMP_PAGES + c], 0))
                  for c in range(CMP_PAGES)]
    grid_spec = pltpu.PrefetchScalarGridSpec(
        num_scalar_prefetch=1, grid=(n_steps,),
        in_specs=page_specs + [pl.BlockSpec(pos.shape, lambda s, pt: (0, 0)),
                               pl.BlockSpec(w1.shape, lambda s, pt: (0, 0)),
                               pl.BlockSpec(w2.shape, lambda s, pt: (0, 0))],
        out_specs=pl.BlockSpec((1, nrow, KVW), lambda s, pt: (s, 0, 0)),
        scratch_shapes=[pltpu.VMEM((NSA_KV, CMP_PAGES * PAGE, HEAD_DIM), F32)])
    return pl.pallas_call(
        _compress_kernel, grid_spec=grid_spec,
        out_shape=jax.ShapeDtypeStruct((n_steps, nrow, KVW), F32),
        compiler_params=_cparams(("arbitrary",)),
        name="compress",
    )(page_ids, *([pages] * CMP_PAGES), pos, w1, w2)


def _softmax_masked(s, mask):
    sm = jnp.where(mask, s, NEG)
    e = jnp.exp(sm - jnp.max(sm, axis=-1, keepdims=True))
    p = e * (1.0 / jnp.sum(e, axis=-1, keepdims=True))
    return jnp.where(mask, p, 0.0)


def _cmp_block_end(n_lanes):
    half = n_lanes // 2
    lane = _iota((1, n_lanes), 1)
    blk = jnp.where(lane < half, 2 * lane, 2 * (lane - half) + 1)
    return (blk + 1) * CMP_BLOCK - 1


def _top_blocks(score, k_top):
    lane = _iota(score.shape, 1)
    work = score
    picked = jnp.zeros(score.shape, jnp.bool_)
    for _ in range(k_top):
        hit = lane == jnp.argmax(work, axis=1, keepdims=True).astype(jnp.int32)
        picked = picked | hit
        work = jnp.where(hit, -jnp.inf, work)
    return picked


def _select_mask(imp, tpos):
    nb = imp.shape[1]
    b = _iota((1, nb), 1)
    cur = tpos >> 6
    forced = (b == 0) | (b == cur) | (b == cur - 1)
    valid = (b * SEL_BLOCK) <= tpos
    score = jnp.where(valid, imp + FORCE_BONUS * forced.astype(F32), NEG)
    return _top_blocks(score, min(N_SEL, nb)) & valid


NSA_TQ = 128
NSA_TK = 1024
NSA_R = NSA_HEADS // NSA_KV
LOG2E = math.log2(math.e)


def _nsa_prompt_kernel(q_ref, gt_ref, kc_ref, vc_ref, ka_ref, vs_ref, kw_ref, vw_ref, o_ref,
                       m_ref, l_ref, acc_ref):
    i = pl.program_id(1)
    s0 = i * NSA_TQ
    rows = NSA_R * NSA_TQ
    q = q_ref[...]
    qr = jnp.concatenate([q[:, r * HEAD_DIM:(r + 1) * HEAD_DIM] for r in range(NSA_R)], axis=0).astype(BF16)
    tpos_q = s0 + _iota((NSA_TQ, 1), 0)
    tpos = jnp.concatenate([tpos_q] * NSA_R, axis=0)

    nc = kc_ref.shape[0]
    nb = nc // 2
    sc = _dot_nt(qr, kc_ref[...].astype(BF16)) * SCALE
    pc = _softmax_masked(sc, _cmp_block_end(nc) <= tpos)
    o_c = _dot(pc.astype(BF16), vc_ref[...].astype(BF16))
    pcs = pc[0:NSA_TQ]
    for r in range(1, NSA_R):
        pcs = pcs + pc[r * NSA_TQ:(r + 1) * NSA_TQ]
    imp = pcs[:, :nb] + pcs[:, nb:]

    wlen = WINDOW + NSA_TQ
    w0 = pl.multiple_of(jnp.maximum(s0 - WINDOW, 0), NSA_TQ)
    sw = _dot_nt(qr, kw_ref[pl.ds(w0, wlen), :]) * SCALE
    kpos = w0 + _iota((1, wlen), 1)
    pw = _softmax_masked(sw, (kpos <= tpos) & (kpos > tpos - WINDOW))
    o_w = _dot(pw.astype(BF16), vw_ref[pl.ds(w0, wlen), :])

    sel = _select_mask(imp, tpos_q)
    bias = jnp.where(sel, 0.0, NEG).astype(BF16)
    q_aug = jnp.concatenate([qr, jnp.concatenate([bias] * NSA_R, axis=0)], axis=1)
    m_ref[...] = jnp.full((rows, 1), M_FLOOR, F32)
    l_ref[...] = jnp.zeros((rows, 1), F32)
    acc_ref[...] = jnp.zeros((rows, HEAD_DIM), F32)

    def tile(k0, tk, masked):
        s = _dot_nt(q_aug, ka_ref[0, pl.ds(k0, tk), :])
        if masked:
            s = jnp.where(k0 + _iota((1, tk), 1) <= tpos, s, NEG)
        m_old = m_ref[...]
        m_new = jnp.maximum(m_old, jnp.max(s, axis=1, keepdims=True))
        p = jnp.exp2((s - m_new) * (SCALE * LOG2E))
        alpha = jnp.exp2((m_old - m_new) * (SCALE * LOG2E))
        l_ref[...] = alpha * l_ref[...] + jnp.sum(p, axis=1, keepdims=True)
        acc_ref[...] = alpha * acc_ref[...] + _dot(p.astype(BF16), vs_ref[pl.ds(k0, tk), :])
        m_ref[...] = m_new

    def full_body(kt, carry):
        tile(pl.multiple_of(kt * NSA_TK, NSA_TK), NSA_TK, False)
        return carry

    n_full = s0 >> 10
    lax.fori_loop(0, n_full, full_body, 0)
    tile(pl.multiple_of(n_full * NSA_TK, NSA_TK), NSA_TK, True)
    o_s = acc_ref[...] * (1.0 / l_ref[...])

    gt = gt_ref[0]
    outs = []
    for r in range(NSA_R):
        rs = slice(r * NSA_TQ, (r + 1) * NSA_TQ)
        outs.append(gt[:, 3 * r:3 * r + 1] * o_c[rs] + gt[:, 3 * r + 1:3 * r + 2] * o_s[rs]
                    + gt[:, 3 * r + 2:3 * r + 3] * o_w[rs])
    o_ref[...] = jnp.concatenate(outs, axis=1).astype(o_ref.dtype)


def _nsa_prompt(z, gates_t, k_cmp, v_cmp, k_aug, kvb, t, vs_col, kw_col, vw_col):
    assert t % NSA_TK == 0 and t >= WINDOW + NSA_TQ
    nc = k_cmp.shape[0]
    rows = NSA_R * NSA_TQ
    return pl.pallas_call(
        _nsa_prompt_kernel, grid=(NSA_KV, t // NSA_TQ),
        in_specs=[pl.BlockSpec((NSA_TQ, NSA_R * HEAD_DIM), lambda g, i: (i, g)),
                  pl.BlockSpec((1, NSA_TQ, 3 * NSA_R), lambda g, i: (g, i, 0)),
                  pl.BlockSpec((nc, HEAD_DIM), lambda g, i: (0, g)),
                  pl.BlockSpec((nc, HEAD_DIM), lambda g, i: (0, g)),
                  pl.BlockSpec((1, t, 2 * HEAD_DIM), lambda g, i: (g, 0, 0)),
                  pl.BlockSpec((t, HEAD_DIM), lambda g, i: (0, vs_col + g)),
                  pl.BlockSpec((t, HEAD_DIM), lambda g, i: (0, kw_col + g)),
                  pl.BlockSpec((t, HEAD_DIM), lambda g, i: (0, vw_col + g))],
        out_specs=pl.BlockSpec((NSA_TQ, NSA_R * HEAD_DIM), lambda g, i: (i, g)),
        out_shape=jax.ShapeDtypeStruct((t, NSA_HEADS * HEAD_DIM), BF16),
        scratch_shapes=[pltpu.VMEM((rows, 1), F32), pltpu.VMEM((rows, 1), F32),
                        pltpu.VMEM((rows, HEAD_DIM), F32)],
        compiler_params=_cparams(("arbitrary", "arbitrary")),
        name="nsa_prompt",
    )(z, gates_t, k_cmp, v_cmp, k_aug, kvb, kvb, kvb)


SB_T = 256
SB_R = SB_HEADS // NSA_KV


def _strict_upper(n):
    return (_iota((n, n), 0) > _iota((n, n), 1)).astype(BF16)


def _log_sigmoids(z):
    lb = jnp.minimum(z, 0.0) - jnp.log(1.0 + jnp.exp(-jnp.abs(z)))
    return lb, lb - z


def _sb_tile(z, causal, carry, u):
    lb, lk = _log_sigmoids(z)
    if causal is not None:
        lk = jnp.where(causal, lk, 0.0)
    after = _dot(lk.astype(BF16), u) + carry
    a = jnp.exp(lb + after)
    if causal is not None:
        a = jnp.where(causal, a, 0.0)
    return a, carry + jnp.sum(lk, axis=1, keepdims=True)


def _sb_prompt_kernel(q_ref, k_ref, v_ref, o_ref, carry_ref, acc_ref):
    i = pl.program_id(1)
    rows = SB_R * SB_T
    q = q_ref[...]
    qr = jnp.concatenate([q[:, r * HEAD_DIM:(r + 1) * HEAD_DIM] for r in range(SB_R)], axis=0)
    u = _strict_upper(SB_T)
    s0 = pl.multiple_of(i * SB_T, SB_T)

    qi = jnp.concatenate([_iota((SB_T, 1), 0)] * SB_R, axis=0)
    causal = _iota((1, SB_T), 1) < qi
    z = _dot_nt(qr, k_ref[pl.ds(s0, SB_T), :])
    a, carry = _sb_tile(z, causal, jnp.zeros((rows, 1), F32), u)
    carry_ref[...] = carry
    acc_ref[...] = _dot(a.astype(BF16), v_ref[pl.ds(s0, SB_T), :])

    def body(kt, c):
        k0 = pl.multiple_of((i - 1 - kt) * SB_T, SB_T)
        zt = _dot_nt(qr, k_ref[pl.ds(k0, SB_T), :])
        at, cn = _sb_tile(zt, None, carry_ref[...], u)
        carry_ref[...] = cn
        acc_ref[...] += _dot(at.astype(BF16), v_ref[pl.ds(k0, SB_T), :])
        return c

    lax.fori_loop(0, i, body, 0)
    acc = acc_ref[...]
    o_ref[...] = jnp.concatenate([acc[r * SB_T:(r + 1) * SB_T] for r in range(SB_R)], axis=1).astype(o_ref.dtype)


def _sb_prompt(sq, sk, sv):
    t = sq.shape[0]
    assert t % SB_T == 0
    rows = SB_R * SB_T
    return pl.pallas_call(
        _sb_prompt_kernel, grid=(NSA_KV, t // SB_T),
        in_specs=[pl.BlockSpec((SB_T, SB_R * HEAD_DIM), lambda g, i: (i, g)),
                  pl.BlockSpec((t, HEAD_DIM), lambda g, i: (0, g)),
                  pl.BlockSpec((t, HEAD_DIM), lambda g, i: (0, g))],
        out_specs=pl.BlockSpec((SB_T, SB_R * HEAD_DIM), lambda g, i: (i, g)),
        out_shape=jax.ShapeDtypeStruct(sq.shape, BF16),
        scratch_shapes=[pltpu.VMEM((rows, 1), F32), pltpu.VMEM((rows, HEAD_DIM), F32)],
        compiler_params=_cparams(("arbitrary", "arbitrary")),
        name="sb_prompt",
    )(sq, sk, sv)


MEM_TQ = 512


def _mem_prompt_kernel(q_ref, k_ref, v_ref, o_ref):
    outs = []
    for h in range(MEM_HEADS):
        hs = slice(h * HEAD_DIM, (h + 1) * HEAD_DIM)
        s = _dot_nt(q_ref[:, hs].astype(BF16), k_ref[:, hs].astype(BF16)) * SCALE
        e = jnp.exp(s - jnp.max(s, axis=-1, keepdims=True))
        p = e * (1.0 / jnp.sum(e, axis=-1, keepdims=True))
        outs.append(_dot(p.astype(BF16), v_ref[:, hs].astype(BF16)))
    o_ref[...] = jnp.concatenate(outs, axis=1).astype(o_ref.dtype)


def _mem_prompt(z, mq_col, mk, mv, t):
    n_mem = mk.shape[0]
    w = MEM_HEADS * HEAD_DIM
    return pl.pallas_call(
        _mem_prompt_kernel, grid=(t // MEM_TQ,),
        in_specs=[pl.BlockSpec((MEM_TQ, w), lambda i: (i, mq_col)),
                  pl.BlockSpec((n_mem, w), lambda i: (0, 0)),
                  pl.BlockSpec((n_mem, w), lambda i: (0, 0))],
        out_specs=pl.BlockSpec((MEM_TQ, w), lambda i: (i, 0)),
        out_shape=jax.ShapeDtypeStruct((t, w), BF16),
        compiler_params=_cparams(("arbitrary",)),
        name="mem_prompt",
    )(z, mk, mv)


HROWS = 16


def _rows_of_group(n_rows, g, heads_per_group):
    row = _iota((n_rows, 1), 0)
    return (row >= g * heads_per_group) & (row < (g + 1) * heads_per_group)


def _group_lanes(full, heads_per_group):
    out = jnp.zeros((full.shape[0], HEAD_DIM), F32)
    for g in range(NSA_KV):
        out = out + jnp.where(_rows_of_group(full.shape[0], g, heads_per_group),
                              full[:, g * HEAD_DIM:(g + 1) * HEAD_DIM], 0.0)
    return out


def _token_rows(ref, n_tok, lead=()):
    return jnp.concatenate([ref[(*lead, pl.ds(g, n_tok, stride=NSA_KV), slice(None))] for g in range(NSA_KV)],
                           axis=1).astype(BF16)


def _page_specs(n_pages):
    return [pl.BlockSpec((PAGE * NSA_KV, HEAD_DIM), lambda b, pt, c=c: (pt[b * n_pages + c], 0))
            for c in range(n_pages)]


def _dec_sb_kernel(pt_ref, q_ref, *refs, n_pages):
    del pt_ref
    k_refs, v_refs, o_ref = refs[:n_pages], refs[n_pages:2 * n_pages], refs[2 * n_pages]
    q = q_ref[0]
    u = _strict_upper(PAGE)
    z = jnp.concatenate([_dot_nt(q, _token_rows(r, PAGE)) for r in k_refs], axis=1)
    lb, lk = _log_sigmoids(z)
    lkb = lk.astype(BF16)
    page = lambda x, c: x[:, c * PAGE:(c + 1) * PAGE]
    carry = jnp.zeros((HROWS, 1), F32)
    acc = jnp.zeros((HROWS, KVW), F32)
    for c in reversed(range(n_pages)):
        after = _dot(page(lkb, c), u) + carry
        a = jnp.exp(page(lb, c) + after)
        acc = acc + _dot(a.astype(BF16), _token_rows(v_refs[c], PAGE))
        carry = carry + jnp.sum(page(lk, c), axis=1, keepdims=True)
    o_ref[0] = _group_lanes(acc, SB_R).astype(o_ref.dtype)


def _dec_sb(page_ids, q_rows, cache_k, cache_v, n_pages):
    nb = q_rows.shape[0]
    grid_spec = pltpu.PrefetchScalarGridSpec(
        num_scalar_prefetch=1, grid=(nb,),
        in_specs=[pl.BlockSpec((1, HROWS, KVW), lambda b, pt: (b, 0, 0))] + 2 * _page_specs(n_pages),
        out_specs=pl.BlockSpec((1, HROWS, HEAD_DIM), lambda b, pt: (b, 0, 0)))
    return pl.pallas_call(
        functools.partial(_dec_sb_kernel, n_pages=n_pages), grid_spec=grid_spec,
        out_shape=jax.ShapeDtypeStruct((nb, HROWS, HEAD_DIM), BF16),
        compiler_params=_cparams(("arbitrary",)),
        name="dec_sb",
    )(page_ids, q_rows, *([cache_k] * n_pages), *([cache_v] * n_pages))


def _bf16_round(x):
    return x.astype(BF16).astype(F32)


def _pad_halves(x, half_rows):
    n = x.shape[0] // 2
    zeros = jnp.zeros((half_rows - n, x.shape[1]), x.dtype)
    return jnp.concatenate([x[:n], zeros, x[n:], zeros], axis=0)


def _dec_nsa_kernel(pt_ref, q_ref, gt_ref, kc_ref, vc_ref, wk_ref, wv_ref, nks_ref, nvs_ref, nkw_ref, nvw_ref,
                    *refs, past, n_pages):
    del pt_ref
    ks_refs, vs_refs, o_ref = refs[:n_pages], refs[n_pages:2 * n_pages], refs[2 * n_pages]
    q = q_ref[0]
    qf = q.astype(F32)

    nb = HEAD_DIM
    kc = _pad_halves(kc_ref[0].astype(BF16), nb)
    vc = _pad_halves(vc_ref[0].astype(BF16), nb)
    sc = _dot_nt(q, kc) * SCALE
    pc = _softmax_masked(sc, _cmp_block_end(2 * nb) <= past)
    o_c = _group_lanes(_dot(pc.astype(BF16), vc), NSA_R)
    imp = jnp.zeros((8, nb), F32)
    for g in range(NSA_KV):
        pcs = jnp.sum(pc[g * NSA_R:(g + 1) * NSA_R], axis=0, keepdims=True)
        imp = imp + jnp.where(_iota((8, 1), 0) == g, pcs[:, :nb] + pcs[:, nb:], 0.0)
    sel = _select_mask(imp, jnp.full((1, 1), past, jnp.int32)).astype(F32)
    pk = jnp.zeros((HROWS, nb), F32)
    for g in range(NSA_KV):
        pk = pk + jnp.where(_rows_of_group(HROWS, g, NSA_R), sel[g:g + 1, :], 0.0)

    s = jnp.concatenate([_dot_nt(q, _token_rows(r, PAGE)) for r in ks_refs], axis=1) * SCALE
    expand = (_iota((nb, past), 0) == (_iota((nb, past), 1) >> 6)).astype(BF16)
    s = jnp.where(_dot(pk.astype(BF16), expand) > 0.5, s, NEG)
    s_new = jnp.sum(qf * _bf16_round(nks_ref[0]), axis=1, keepdims=True) * SCALE
    nblk = past // SEL_BLOCK
    s_new = jnp.where(pk[:, nblk:nblk + 1] > 0.5, s_new, NEG)
    ms = jnp.maximum(jnp.maximum(jnp.max(s, axis=1, keepdims=True), s_new), M_FLOOR)
    e = jnp.exp(s - ms)
    e_new = jnp.exp(s_new - ms)
    ls = jnp.sum(e, axis=1, keepdims=True) + e_new
    acc = _bf16_round(e_new) * _bf16_round(nvs_ref[0])
    for c in range(n_pages):
        acc = acc + _dot(e[:, c * PAGE:(c + 1) * PAGE].astype(BF16), _token_rows(vs_refs[c], PAGE))
    o_s = _group_lanes(acc, NSA_R) * (1.0 / ls)

    wl = wk_ref.shape[1] // NSA_KV
    sw = _dot_nt(q, _token_rows(wk_ref, wl, (0,))) * SCALE
    wmask = past - wl + _iota((1, wl), 1) > past - WINDOW
    sw = jnp.where(wmask, sw, NEG)
    sw_new = jnp.sum(qf * _bf16_round(nkw_ref[0]), axis=1, keepdims=True) * SCALE
    mw = jnp.maximum(jnp.max(sw, axis=1, keepdims=True), sw_new)
    ew = jnp.where(wmask, jnp.exp(sw - mw), 0.0)
    ew_new = jnp.exp(sw_new - mw)
    lw = jnp.sum(ew, axis=1, keepdims=True) + ew_new
    accw = _dot(ew.astype(BF16), _token_rows(wv_ref, wl, (0,))) + _bf16_round(ew_new) * _bf16_round(nvw_ref[0])
    o_w = _group_lanes(accw, NSA_R) * (1.0 / lw)

    gt = gt_ref[0]
    o_ref[0] = (gt[:, 0:1] * o_c + gt[:, 1:2] * o_s + gt[:, 2:3] * o_w).astype(o_ref.dtype)


def _dec_nsa(page_ids, q_rows, gates, k_cmp, v_cmp, cache_k, cache_v, win_k, win_v, new_ks, new_vs, new_kw,
             new_vw, n_pages):
    nb = q_rows.shape[0]
    nc = k_cmp.shape[1]
    past = n_pages * PAGE
    assert nc <= 2 * HEAD_DIM and past // SEL_BLOCK < HEAD_DIM
    per_b = lambda shape: pl.BlockSpec((1,) + shape, lambda b, pt: (b, 0, 0))
    grid_spec = pltpu.PrefetchScalarGridSpec(
        num_scalar_prefetch=1, grid=(nb,),
        in_specs=[per_b((HROWS, KVW)), per_b((HROWS, 3)), per_b((nc, KVW)), per_b((nc, KVW)),
                  per_b(win_k.shape[1:]), per_b(win_v.shape[1:]),
                  per_b((1, KVW)), per_b((1, KVW)), per_b((1, KVW)), per_b((1, KVW))] + 2 * _page_specs(n_pages),
        out_specs=per_b((HROWS, HEAD_DIM)))
    return pl.pallas_call(
        functools.partial(_dec_nsa_kernel, past=past, n_pages=n_pages), grid_spec=grid_spec,
        out_shape=jax.ShapeDtypeStruct((nb, HROWS, HEAD_DIM), BF16),
        compiler_params=_cparams(("arbitrary",)),
        name="dec_nsa",
    )(page_ids, q_rows, gates, k_cmp, v_cmp, win_k, win_v, new_ks, new_vs, new_kw, new_vw,
      *([cache_k] * n_pages), *([cache_v] * n_pages))


def _dec_mem_kernel(q_ref, k_ref, v_ref, o_ref):
    n_mem = k_ref.shape[1] // NSA_KV
    s = _dot_nt(q_ref[0], _token_rows(k_ref, n_mem, (0,))) * SCALE
    e = jnp.exp(s - jnp.max(s, axis=-1, keepdims=True))
    p = e * (1.0 / jnp.sum(e, axis=-1, keepdims=True))
    o_ref[0] = _group_lanes(_dot(p.astype(BF16), _token_rows(v_ref, n_mem, (0,))), 1).astype(o_ref.dtype)


def _dec_mem(q_rows, mem_k, mem_v):
    nb = mem_k.shape[0]
    rows = q_rows.shape[1]
    per_b = lambda shape: pl.BlockSpec((1,) + shape, lambda b: (b, 0, 0))
    return pl.pallas_call(
        _dec_mem_kernel, grid=(nb,),
        in_specs=[per_b((rows, KVW)), per_b(mem_k.shape[1:]), per_b(mem_v.shape[1:])],
        out_specs=per_b((rows, HEAD_DIM)),
        out_shape=jax.ShapeDtypeStruct((nb, rows, HEAD_DIM), BF16),
        compiler_params=_cparams(("arbitrary",)),
        name="dec_mem",
    )(q_rows, mem_k, mem_v)


def _head_rows(q, n_heads, heads_per_group, rows):
    nb = q.shape[0]
    qh = q.reshape(nb, n_heads, 1, HEAD_DIM)
    onehot = (jnp.arange(n_heads)[:, None] // heads_per_group == jnp.arange(NSA_KV)[None, :])
    full = jnp.where(onehot[None, :, :, None], qh, 0.0).reshape(nb, n_heads, KVW)
    return jnp.pad(full, ((0, 0), (0, rows - n_heads), (0, 0))).astype(BF16)


def _even_odd(x, lanes_half):
    n = x.shape[-2]
    pad = [(0, 0)] * (x.ndim - 2) + [(0, lanes_half - n // 2), (0, 0)]
    return jnp.concatenate([jnp.pad(x[..., 0::2, :], pad), jnp.pad(x[..., 1::2, :], pad)], axis=-2)


def _rope_tables(pos):
    half = ROPE_DIMS // 2
    inv = jnp.power(ROPE_THETA, -jnp.arange(half, dtype=F32) * 2.0 / ROPE_DIMS)
    ang = pos.astype(F32)[:, None] * inv[None, :]
    cos, sin = jnp.cos(ang), jnp.sin(ang)
    n = pos.shape[0]
    rest = HEAD_DIM - ROPE_DIMS
    c = jnp.concatenate([cos, cos, jnp.ones((n, rest), F32)], axis=1)
    s_lo = jnp.concatenate([-sin, jnp.zeros((n, HEAD_DIM - half), F32)], axis=1)
    s_hi = jnp.concatenate([jnp.zeros((n, half), F32), sin, jnp.zeros((n, rest), F32)], axis=1)
    return c, s_lo, s_hi


C_Q, C_KC, C_VC, C_KS, C_VS, C_KW, C_VW = 0, 2048, 2560, 3072, 3584, 4096, 4608
C_SQ, C_SK, C_SV, C_MQ, C_BG, C_END = 5120, 6656, 7168, 7680, 8192, 20480
NG_COLS = NSA_HEADS * 3
NG_SRC = 5120


def kernel(x_prompt, mem_prompt, x_sample, cache_cmp_k, cache_cmp_v, cache_sel_k, cache_sel_v, cache_sb_k,
           cache_sb_v, state_win_k, state_win_v, cache_mem_k, cache_mem_v, page_table, norm1_w, w_in, g_nsa_q,
           g_cmp_k, g_sel_k, g_win_k, g_mem_q, g_mem_k, mem_norm_w, w_mem_kv, cmp_pos_k, cmp_k_w1, cmp_k_w2,
           cmp_pos_v, cmp_v_w1, cmp_v_w2, w_up_nsa, w_up_sb, w_up_mem, w_o, norm2_w, w_ff1, w_ff2):
    tp = x_prompt.shape[1]
    nb = x_sample.shape[0]
    n_pages = page_table.shape[1]
    past = n_pages * PAGE
    m = tp + nb
    tm = _row_tile(m, 832)
    assert x_prompt.shape[0] == 1 and x_sample.shape[1] == 1

    h = _rms_stack(x_prompt[0], x_sample[:, 0], norm1_w)
    w_a = w_in[:, :NG_SRC].astype(BF16)
    w_b = w_in[:, NG_SRC + NG_COLS:].astype(BF16)
    w_ng = jnp.pad(w_in[:, NG_SRC:NG_SRC + NG_COLS], ((0, 0), (0, HEAD_DIM - NG_COLS))).astype(BF16)
    pos = jnp.concatenate([jnp.arange(tp, dtype=jnp.int32), jnp.full((nb,), past, jnp.int32)])
    rope_c, rope_lo, rope_hi = _rope_tables(pos)
    modes = np.zeros((C_END // IN_TN,), np.int32)
    gidx = np.zeros((C_END // IN_TN,), np.int32)
    for c0, c1, md, gi in ((C_Q, C_KC, MODE_NORM_ROPE, 0), (C_KC, C_VC, MODE_NORM_ROPE, 1),
                           (C_KS, C_VS, MODE_NORM_ROPE, 2), (C_KW, C_VW, MODE_NORM_ROPE, 3),
                           (C_MQ, C_BG, MODE_NORM, 4), (C_BG, C_END, MODE_SIGMOID, 0)):
        modes[c0 // IN_TN:c1 // IN_TN] = md
        gidx[c0 // IN_TN:c1 // IN_TN] = gi
    gains = jnp.stack([g_nsa_q, g_cmp_k, g_sel_k, g_win_k, g_mem_q]).reshape(5, 1, HEAD_DIM)
    na = C_SQ // IN_TN
    z_a = _inproj(h, w_a, jnp.asarray(modes[:na]), jnp.asarray(gidx[:na]), gains, rope_c, rope_lo, rope_hi, tm)
    z_b = _inproj(h, w_b, jnp.asarray(modes[na:]), jnp.asarray(gidx[na:]), gains, rope_c, rope_lo, rope_hi, tm)
    ng = _mm(h, w_ng, tm=tm, tn=HEAD_DIM, tk=D_MODEL, out_dtype=F32, epi=_epi_sigmoid, name="head_gates")

    def zcols(c0, c1, r0, r1):
        arr, off = (z_a, 0) if c0 < C_SQ else (z_b, C_SQ)
        return arr[r0:r1, c0 - off:c1 - off]

    zp = lambda c0, c1: zcols(c0, c1, 0, tp)
    zs = lambda c0, c1: zcols(c0, c1, tp, m)
    kv_out = lambda a: a.reshape(a.shape[0], NSA_KV, HEAD_DIM)
    kv_cols = (C_KC, C_VC, C_KS, C_VS, C_SK, C_SV)
    p_kv = [kv_out(zp(c, c + KVW))[None] for c in kv_cols]
    s_kv = [kv_out(zs(c, c + KVW))[:, None] for c in kv_cols]
    keep = min(WINDOW, tp)
    p_win = [kv_out(zcols(c, c + KVW, tp - keep, tp))[None] for c in (C_KW, C_VW)]
    new_win = [kv_out(zs(c, c + KVW))[:, None] for c in (C_KW, C_VW)]
    lb = state_win_k.shape[1]
    keep_s = min(WINDOW, past + 1)
    s_win = [jnp.concatenate([st, nw], axis=1)[:, lb + 1 - keep_s:]
             for st, nw in ((state_win_k, new_win[0]), (state_win_v, new_win[1]))]

    w1k, w2k = cmp_k_w1.astype(BF16), cmp_k_w2.astype(BF16)
    w1v, w2v = cmp_v_w1.astype(BF16), cmp_v_w2.astype(BF16)
    ids_p = jnp.arange(tp // PAGE, dtype=jnp.int32)
    n_cmp = tp // CMP_BLOCK
    token_rows = lambda a: a.reshape(-1, HEAD_DIM)
    k_cmp_p = _even_odd(_compress(token_rows(p_kv[0]), ids_p, cmp_pos_k, w1k, w2k).reshape(n_cmp, KVW), HEAD_DIM)
    v_cmp_p = _even_odd(_compress(token_rows(p_kv[1]), ids_p, cmp_pos_v, w1v, w2v).reshape(n_cmp, KVW), HEAD_DIM)

    gates_all = ng[:, :NG_COLS].reshape(m, NSA_KV, 3 * NSA_R)
    gates_t = jnp.transpose(gates_all[:tp], (1, 0, 2))
    kvb = zp(C_KC, C_SQ).astype(BF16)
    blk_of_key = jnp.arange(tp, dtype=jnp.int32) // SEL_BLOCK
    onehot = (blk_of_key[:, None] == jnp.arange(HEAD_DIM, dtype=jnp.int32)[None, :]).astype(BF16)
    ks_b = jnp.transpose(zp(C_KS, C_VS).astype(BF16).reshape(tp, NSA_KV, HEAD_DIM), (1, 0, 2))
    k_aug = jnp.concatenate([ks_b, jnp.broadcast_to(onehot[None], ks_b.shape)], axis=2)
    cb = lambda c0: (c0 - C_KC) // HEAD_DIM
    o_nsa_p = _nsa_prompt(z_a, gates_t, k_cmp_p, v_cmp_p, k_aug, kvb, tp, cb(C_VS), cb(C_KW), cb(C_VW))

    o_sb_p = _sb_prompt((zp(C_SQ, C_SK) * SCALE).astype(BF16), zp(C_SK, C_SV).astype(BF16),
                        zp(C_SV, C_MQ).astype(BF16))

    mem_h = _rms(mem_prompt[0], mem_norm_w, mem_prompt.shape[1])
    w_mkv = w_mem_kv.astype(BF16)
    n_mem = mem_prompt.shape[1]
    mw = MEM_HEADS * HEAD_DIM
    mk = _mm(mem_h, w_mkv, tm=n_mem, tn=mw, tk=D_MODEL, out_dtype=F32, epi=_epi_headnorm,
             extras=((g_mem_k.reshape(1, HEAD_DIM), 0),), ncols=mw, name="mem_k")
    mv = _mm(mem_h, w_mkv, tm=n_mem, tn=mw, tk=D_MODEL, out_dtype=F32, col0=1, name="mem_v")
    o_mem_p = _mem_prompt(z_b, (C_MQ - C_SQ) // mw, mk, mv, tp)

    pt = page_table.reshape(-1).astype(jnp.int32)
    n_cmp_s = past // CMP_BLOCK
    k_cmp_s = _even_odd(_compress(token_rows(cache_cmp_k), pt, cmp_pos_k, w1k, w2k), n_cmp_s // 2)
    v_cmp_s = _even_odd(_compress(token_rows(cache_cmp_v), pt, cmp_pos_v, w1v, w2v), n_cmp_s // 2)
    q_nsa_s = _head_rows(zs(C_Q, C_KC), NSA_HEADS, NSA_R, HROWS)
    gates_s = ng[tp:, :NG_COLS].reshape(nb, NSA_HEADS, 3)
    seq_rows = lambda a: a.reshape(a.shape[0], -1, HEAD_DIM)
    new_row = lambda c0: zs(c0, c0 + KVW).reshape(nb, 1, KVW)
    o_nsa_s = _dec_nsa(pt, q_nsa_s, gates_s, k_cmp_s, v_cmp_s, token_rows(cache_sel_k), token_rows(cache_sel_v),
                       seq_rows(state_win_k), seq_rows(state_win_v), new_row(C_KS), new_row(C_VS), new_row(C_KW),
                       new_row(C_VW), n_pages)
    q_sb_s = _head_rows(zs(C_SQ, C_SK) * SCALE, SB_HEADS, SB_R, HROWS)
    o_sb_s = _dec_sb(pt, q_sb_s, token_rows(cache_sb_k), token_rows(cache_sb_v), n_pages)
    q_mem_s = _head_rows(zs(C_MQ, C_BG), MEM_HEADS, 1, 8)
    o_mem_s = _dec_mem(q_mem_s, seq_rows(cache_mem_k), seq_rows(cache_mem_v))

    o_nsa = jnp.concatenate([o_nsa_p, o_nsa_s.reshape(nb, NSA_HEADS * HEAD_DIM)], axis=0)
    o_sb = jnp.concatenate([o_sb_p, o_sb_s[:, :SB_HEADS].reshape(nb, SB_HEADS * HEAD_DIM)], axis=0)
    o_mem = jnp.concatenate([o_mem_p, o_mem_s[:, :MEM_HEADS].reshape(nb, mw)], axis=0)

    u = _merge_up(o_nsa, o_sb, o_mem, w_up_nsa.astype(BF16), w_up_sb.astype(BF16), w_up_mem.astype(BF16),
                  z_b, C_BG - C_SQ, tm, 512)
    x_all = jnp.concatenate([x_prompt[0], x_sample[:, 0]], axis=0)
    x1 = _mm(u, w_o.astype(BF16), tm=tm, tn=1024, tk=D_MODEL, out_dtype=F32, epi=_epi_residual,
             extras=((x_all, 0),), name="out_proj")
    h2 = _rms(x1, norm2_w, _row_tile(m, 512))
    a = _mm(h2, w_ff1.astype(BF16), tm=tm, tn=1024, tk=D_MODEL, out_dtype=BF16, epi=_epi_relu2, name="ff1")
    y = _mm(a, w_ff2.astype(BF16), tm=tm, tn=1024, tk=4096, out_dtype=F32, epi=_epi_residual,
            extras=((x1, 0),), name="ff2")

    y_prompt = y[:tp][None]
    y_sample = y[tp:][:, None]
    p_mem = [mk.reshape(1, n_mem, MEM_HEADS, HEAD_DIM), mv.reshape(1, n_mem, MEM_HEADS, HEAD_DIM)]
    return (y_prompt, y_sample, *p_kv, *p_win, *p_mem, *s_kv, *s_win)
```

```python
import functools
import math

import numpy as np
import jax
import jax.numpy as jnp
from jax import lax
from jax.experimental import pallas as pl
from jax.experimental.pallas import tpu as pltpu

F32 = jnp.float32
BF16 = jnp.bfloat16

D_MODEL = 4096
HEAD_DIM = 128
NSA_HEADS = 16
NSA_KV = 4
SB_HEADS = 12
MEM_HEADS = 4
CMP_BLOCK = 32
SEL_BLOCK = 64
N_SEL = 16
WINDOW = 512
PAGE = 128
ROPE_THETA = 500000.0
ROPE_DIMS = HEAD_DIM // 4
EPS = 1e-6
NEG = -1e30
M_FLOOR = -1e20
FORCE_BONUS = 1e4
SCALE = HEAD_DIM ** -0.5
KVW = NSA_KV * HEAD_DIM

VMEM_LIMIT = 56 * 1024 * 1024


def _row_tile(m, cap):
    for step in (16, 8):
        best = max((t for t in range(step, cap + 1, step) if m % t == 0), default=0)
        if best:
            return best
    raise ValueError(f"no row tile for {m}")


def _cparams(sem, vmem=VMEM_LIMIT):
    return pltpu.CompilerParams(dimension_semantics=sem, vmem_limit_bytes=vmem)


def _dot(a, b):
    return jnp.dot(a, b, preferred_element_type=F32)


def _dot_nt(a, b):
    return lax.dot_general(a, b, (((1,), (1,)), ((), ())), preferred_element_type=F32)


def _iota(shape, dim):
    return lax.broadcasted_iota(jnp.int32, shape, dim)


def _rms2_kernel(xp_ref, xs_ref, w_ref, o_ref, *, n_prompt_tiles):
    i = pl.program_id(0)
    x = jnp.where(i < n_prompt_tiles, xp_ref[...], xs_ref[...])
    ms = jnp.mean(x * x, axis=-1, keepdims=True)
    o_ref[...] = (x * lax.rsqrt(ms + EPS) * w_ref[...]).astype(o_ref.dtype)


def _rms_stack(xp, xs, w):
    tp, d = xp.shape
    ts = xs.shape[0]
    assert tp % ts == 0 and ts % 8 == 0
    npt = tp // ts
    return pl.pallas_call(
        functools.partial(_rms2_kernel, n_prompt_tiles=npt),
        grid=(npt + 1,),
        in_specs=[pl.BlockSpec((ts, d), lambda i: (jnp.minimum(i, npt - 1), 0)),
                  pl.BlockSpec((ts, d), lambda i: (0, 0)),
                  pl.BlockSpec((1, d), lambda i: (0, 0))],
        out_specs=pl.BlockSpec((ts, d), lambda i: (i, 0)),
        out_shape=jax.ShapeDtypeStruct((tp + ts, d), BF16),
        compiler_params=_cparams(("arbitrary",)),
        name="rms_stack",
    )(xp, xs, w.reshape(1, d))


def _rms_kernel(x_ref, w_ref, o_ref):
    x = x_ref[...]
    ms = jnp.mean(x * x, axis=-1, keepdims=True)
    o_ref[...] = (x * lax.rsqrt(ms + EPS) * w_ref[...]).astype(o_ref.dtype)


def _rms(x, w, tm):
    m, d = x.shape
    return pl.pallas_call(
        _rms_kernel, grid=(m // tm,),
        in_specs=[pl.BlockSpec((tm, d), lambda i: (i, 0)), pl.BlockSpec((1, d), lambda i: (0, 0))],
        out_specs=pl.BlockSpec((tm, d), lambda i: (i, 0)),
        out_shape=jax.ShapeDtypeStruct((m, d), BF16),
        compiler_params=_cparams(("arbitrary",)),
        name="rms",
    )(x, w.reshape(1, d))


MODE_PLAIN, MODE_NORM_ROPE, MODE_NORM = 0, 1, 2
IN_TN = 512
IN_PARTS = 2


def _head_norm(z, g):
    outs = []
    for hh in range(z.shape[1] // HEAD_DIM):
        xs = z[:, hh * HEAD_DIM:(hh + 1) * HEAD_DIM]
        ms = jnp.mean(xs * xs, axis=-1, keepdims=True)
        outs.append(xs * lax.rsqrt(ms + EPS) * g)
    return outs


def _rope(x, c, s_lo, s_hi):
    half = ROPE_DIMS // 2
    return x * c + pltpu.roll(x, HEAD_DIM - half, 1) * s_lo + pltpu.roll(x, half, 1) * s_hi


def _inproj_kernel(mode_ref, gidx_ref, a_ref, b_ref, g_ref, c_ref, slo_ref, shi_ref, o_ref):
    j = pl.program_id(1)
    o_ref[...] = _dot(a_ref[...], b_ref[...])
    for part in range(IN_PARTS):
        cols = slice(part * IN_TN, (part + 1) * IN_TN)
        mode = mode_ref[j * IN_PARTS + part]
        gain = g_ref[gidx_ref[j * IN_PARTS + part]]

        @pl.when(mode == MODE_NORM)
        def _():
            o_ref[:, cols] = jnp.concatenate(_head_norm(o_ref[:, cols], gain), axis=1)

        @pl.when(mode == MODE_NORM_ROPE)
        def _():
            c, slo, shi = c_ref[...], slo_ref[...], shi_ref[...]
            o_ref[:, cols] = jnp.concatenate([_rope(y, c, slo, shi) for y in _head_norm(o_ref[:, cols], gain)],
                                             axis=1)


def _inproj(h, w, modes, gidx, gains, rope_c, rope_lo, rope_hi, tm):
    m, k = h.shape
    n = w.shape[1]
    tn = IN_PARTS * IN_TN
    grid_spec = pltpu.PrefetchScalarGridSpec(
        num_scalar_prefetch=2, grid=(m // tm, n // tn),
        in_specs=[pl.BlockSpec((tm, k), lambda i, j, *_: (i, 0)),
                  pl.BlockSpec((k, tn), lambda i, j, *_: (0, j)),
                  pl.BlockSpec(gains.shape, lambda i, j, *_: (0, 0, 0)),
                  pl.BlockSpec((tm, HEAD_DIM), lambda i, j, *_: (i, 0)),
                  pl.BlockSpec((tm, HEAD_DIM), lambda i, j, *_: (i, 0)),
                  pl.BlockSpec((tm, HEAD_DIM), lambda i, j, *_: (i, 0))],
        out_specs=pl.BlockSpec((tm, tn), lambda i, j, *_: (i, j)))
    return pl.pallas_call(
        _inproj_kernel, grid_spec=grid_spec,
        out_shape=jax.ShapeDtypeStruct((m, n), F32),
        compiler_params=_cparams(("arbitrary", "arbitrary")),
        name="inproj",
    )(modes, gidx, h, w, gains, rope_c, rope_lo, rope_hi)


def _mm_kernel(*refs, nk, n_extra, epi):
    a_ref, b_ref = refs[0], refs[1]
    extra = refs[2:2 + n_extra]
    o_ref = refs[2 + n_extra]
    part = _dot(a_ref[...], b_ref[...])
    if nk == 1:
        o_ref[...] = epi(part, *[e[...] for e in extra]).astype(o_ref.dtype)
    else:
        acc_ref = refs[3 + n_extra]
        kk = pl.program_id(2)

        @pl.when(kk == 0)
        def _():
            acc_ref[...] = part

        @pl.when(kk > 0)
        def _():
            acc_ref[...] += part

        @pl.when(kk == nk - 1)
        def _():
            o_ref[...] = epi(acc_ref[...], *[e[...] for e in extra]).astype(o_ref.dtype)


def _mm(a, b, *, tm, tn, tk, out_dtype, epi=None, extras=(), col0=0, ncols=None, name="mm"):
    m, k = a.shape
    ncols = b.shape[1] - col0 * tn if ncols is None else ncols
    assert m % tm == 0 and ncols % tn == 0 and k % tk == 0
    nk = k // tk
    if epi is None:
        epi = lambda z: z
    in_specs = [pl.BlockSpec((tm, tk), lambda i, j, kk: (i, kk)),
                pl.BlockSpec((tk, tn), lambda i, j, kk: (kk, j + col0))]
    args = [a, b]
    for arr, off in extras:
        if arr.shape[0] == 1:
            in_specs.append(pl.BlockSpec((1, arr.shape[1]), lambda i, j, kk: (0, 0)))
        else:
            in_specs.append(pl.BlockSpec((tm, tn), lambda i, j, kk, off=off: (i, j + off)))
        args.append(arr)
    scratch = [pltpu.VMEM((tm, tn), F32)] if nk > 1 else []
    return pl.pallas_call(
        functools.partial(_mm_kernel, nk=nk, n_extra=len(extras), epi=epi),
        grid=(m // tm, ncols // tn, nk),
        in_specs=in_specs,
        out_specs=pl.BlockSpec((tm, tn), lambda i, j, kk: (i, j)),
        out_shape=jax.ShapeDtypeStruct((m, ncols), out_dtype),
        scratch_shapes=scratch,
        compiler_params=_cparams(("arbitrary", "arbitrary", "arbitrary")),
        name=name,
    )(*args)


def _epi_headnorm(z, g):
    return jnp.concatenate(_head_norm(z, g), axis=1)


def _epi_sigmoid(z):
    return jax.nn.sigmoid(z)


def _epi_residual(z, x):
    return x + z


def _epi_relu2(z):
    return jnp.square(jnp.maximum(z, 0.0))


def _up_kernel(on_ref, os_ref, om_ref, wn_ref, ws_ref, wm_ref, g0_ref, g1_ref, g2_ref, o_ref):
    sig = jax.nn.sigmoid
    u = (sig(g0_ref[...]) * _dot(on_ref[...], wn_ref[...])
         + sig(g1_ref[...]) * _dot(os_ref[...], ws_ref[...])
         + sig(g2_ref[...]) * _dot(om_ref[...], wm_ref[...]))
    o_ref[...] = u.astype(o_ref.dtype)


def _merge_up(o_nsa, o_sb, o_mem, w_n, w_s, w_m, z, gate_col0, tm, tn):
    m = o_nsa.shape[0]
    d = w_n.shape[1]
    gb = gate_col0 // tn
    nd = d // tn
    row = lambda kdim: pl.BlockSpec((tm, kdim), lambda i, j: (i, 0))
    col = lambda kdim: pl.BlockSpec((kdim, tn), lambda i, j: (0, j))
    gate = lambda c: pl.BlockSpec((tm, tn), lambda i, j, c=c: (i, gb + c * nd + j))
    return pl.pallas_call(
        _up_kernel, grid=(m // tm, nd),
        in_specs=[row(o_nsa.shape[1]), row(o_sb.shape[1]), row(o_mem.shape[1]),
                  col(w_n.shape[0]), col(w_s.shape[0]), col(w_m.shape[0]),
                  gate(0), gate(1), gate(2)],
        out_specs=pl.BlockSpec((tm, tn), lambda i, j: (i, j)),
        out_shape=jax.ShapeDtypeStruct((m, d), BF16),
        compiler_params=_cparams(("arbitrary", "arbitrary")),
        name="merge_up",
    )(o_nsa, o_sb, o_mem, w_n, w_s, w_m, z, z, z)


CMP_PAGES = 16
CMP_HID = 256
CMP_PITCH = 136


def _gelu_tanh(x):
    cdf = 0.5 * (1.0 + jnp.tanh(math.sqrt(2.0 / math.pi) * (x + 0.044715 * (x * x * x))))
    return x * cdf


def _compress_kernel(pt_ref, *refs):
    del pt_ref
    page_refs = refs[:CMP_PAGES]
    pos_ref, w1_ref, w2_ref, o_ref, buf_ref = refs[CMP_PAGES:]
    brow = CMP_BLOCK * NSA_KV
    for c in range(CMP_PAGES):
        for n in range(PAGE // CMP_BLOCK):
            blk = c * (PAGE // CMP_BLOCK) + n
            buf_ref[blk * CMP_PITCH:blk * CMP_PITCH + brow, :] = page_refs[c][n * brow:(n + 1) * brow, :]
    nrow = CMP_PAGES * (PAGE // CMP_BLOCK)
    acc = jnp.zeros((NSA_KV * nrow, CMP_HID), F32)
    def token(t):
        parts = [buf_ref[pl.ds(t * NSA_KV + g, nrow, stride=CMP_PITCH), :] for g in range(NSA_KV)]
        return (jnp.concatenate(parts, axis=0) + pos_ref[t:t + 1, :]).astype(BF16)

    accs = [acc, acc]
    for i, t in enumerate(range(0, CMP_BLOCK, 2)):
        x = jnp.concatenate([token(t), token(t + 1)], axis=1)
        accs[i % 2] = accs[i % 2] + _dot(x, w1_ref[t * HEAD_DIM:(t + 2) * HEAD_DIM, :])
    acc = accs[0] + accs[1]
    out = _dot(_gelu_tanh(acc).astype(BF16), w2_ref[...])
    for g in range(NSA_KV):
        o_ref[0, :, g * HEAD_DIM:(g + 1) * HEAD_DIM] = out[g * nrow:(g + 1) * nrow, :]


def _compress(pages, page_ids, pos, w1, w2):
    n_steps = page_ids.shape[0] // CMP_PAGES
    nrow = CMP_PAGES * (PAGE // CMP_BLOCK)
    page_specs = [pl.BlockSpec((PAGE * NSA_KV, HEAD_DIM), lambda s, pt, c=c: (pt[s * CMP_PAGES + c], 0))
                  for c in range(CMP_PAGES)]
    grid_spec = pltpu.PrefetchScalarGridSpec(
        num_scalar_prefetch=1, grid=(n_steps,),
        in_specs=page_specs + [pl.BlockSpec(pos.shape, lambda s, pt: (0, 0)),
                               pl.BlockSpec(w1.shape, lambda s, pt: (0, 0)),
                               pl.BlockSpec(w2.shape, lambda s, pt: (0, 0))],
        out_specs=pl.BlockSpec((1, nrow, KVW), lambda s, pt: (s, 0, 0)),
        scratch_shapes=[pltpu.VMEM((nrow * CMP_PITCH, HEAD_DIM), F32)])
    return pl.pallas_call(
        _compress_kernel, grid_spec=grid_spec,
        out_shape=jax.ShapeDtypeStruct((n_steps, nrow, KVW), F32),
        compiler_params=_cparams(("arbitrary",)),
        name="compress",
    )(page_ids, *([pages] * CMP_PAGES), pos, w1, w2)


def _softmax_masked(s, mask):
    sm = jnp.where(mask, s, NEG)
    e = jnp.exp(sm - jnp.max(sm, axis=-1, keepdims=True))
    p = e * (1.0 / jnp.sum(e, axis=-1, keepdims=True))
    return jnp.where(mask, p, 0.0)


def _cmp_block_end(n_lanes):
    half = n_lanes // 2
    lane = _iota((1, n_lanes), 1)
    blk = jnp.where(lane < half, 2 * lane, 2 * (lane - half) + 1)
    return (blk + 1) * CMP_BLOCK - 1


def _top_blocks(score, k_top):
    lane = _iota(score.shape, 1)
    work = score
    picked = jnp.zeros(score.shape, jnp.bool_)
    for _ in range(k_top):
        hit = lane == jnp.argmax(work, axis=1, keepdims=True).astype(jnp.int32)
        picked = picked | hit
        work = jnp.where(hit, -jnp.inf, work)
    return picked


def _select_mask(imp, tpos):
    nb = imp.shape[1]
    b = _iota((1, nb), 1)
    cur = tpos >> 6
    forced = (b == 0) | (b == cur) | (b == cur - 1)
    valid = (b * SEL_BLOCK) <= tpos
    score = jnp.where(valid, imp + FORCE_BONUS * forced.astype(F32), NEG)
    return _top_blocks(score, min(N_SEL, nb)) & valid


NSA_TQ = 128
NSA_TK = 1024
NSA_R = NSA_HEADS // NSA_KV
LOG2E = math.log2(math.e)


def _nsa_prompt_kernel(q_ref, gt_ref, kc_ref, vc_ref, ka_ref, vst_ref, kw_ref, vw_ref, o_ref,
                       m_ref, l_ref, acc_ref):
    i = pl.program_id(1)
    s0 = i * NSA_TQ
    rows = NSA_R * NSA_TQ
    q = q_ref[...]
    qr = jnp.concatenate([q[:, r * HEAD_DIM:(r + 1) * HEAD_DIM] for r in range(NSA_R)], axis=0).astype(BF16)
    tpos_q = s0 + _iota((NSA_TQ, 1), 0)
    tpos = jnp.concatenate([tpos_q] * NSA_R, axis=0)

    nc = kc_ref.shape[0]
    nb = nc // 2
    sc = _dot_nt(qr, kc_ref[...].astype(BF16)) * SCALE
    pc = _softmax_masked(sc, _cmp_block_end(nc) <= tpos)
    o_c = _dot(pc.astype(BF16), vc_ref[...].astype(BF16))
    pcs = pc[0:NSA_TQ]
    for r in range(1, NSA_R):
        pcs = pcs + pc[r * NSA_TQ:(r + 1) * NSA_TQ]
    imp = pcs[:, :nb] + pcs[:, nb:]

    wlen = WINDOW + NSA_TQ
    w0 = pl.multiple_of(jnp.maximum(s0 - WINDOW, 0), NSA_TQ)
    sw = _dot_nt(qr, kw_ref[pl.ds(w0, wlen), :]) * SCALE
    kpos = w0 + _iota((1, wlen), 1)
    pw = _softmax_masked(sw, (kpos <= tpos) & (kpos > tpos - WINDOW))
    o_w = _dot(pw.astype(BF16), vw_ref[pl.ds(w0, wlen), :])

    sel = _select_mask(imp, tpos_q)
    bias_t = jnp.where(sel, 0.0, NEG).T
    q_t = jnp.concatenate([q[:, r * HEAD_DIM:(r + 1) * HEAD_DIM].T for r in range(NSA_R)], axis=1)
    q_aug_t = jnp.concatenate([q_t, jnp.concatenate([bias_t] * NSA_R, axis=1)], axis=0).astype(BF16)
    tpos_l = s0 + (_iota((1, rows), 1) & (NSA_TQ - 1))
    m_ref[...] = jnp.full((1, rows), M_FLOOR, F32)
    l_ref[...] = jnp.zeros((1, rows), F32)
    acc_ref[...] = jnp.zeros((HEAD_DIM, rows), F32)

    def tile(kt, masked):
        s = _dot(ka_ref[0, pl.ds(pl.multiple_of(kt * NSA_TK, NSA_TK), NSA_TK), :], q_aug_t)
        if masked:
            s = jnp.where(kt * NSA_TK + _iota((NSA_TK, 1), 0) <= tpos_l, s, NEG)
        m_old = m_ref[...]
        m_new = jnp.maximum(m_old, jnp.max(s, axis=0, keepdims=True))
        p = jnp.exp2((s - m_new) * (SCALE * LOG2E))
        alpha = jnp.exp2((m_old - m_new) * (SCALE * LOG2E))
        l_ref[...] = alpha * l_ref[...] + jnp.sum(p, axis=0, keepdims=True)
        acc_ref[...] = alpha * acc_ref[...] + _dot(vst_ref[0, kt], p.astype(BF16))
        m_ref[...] = m_new

    def full_body(kt, carry):
        tile(kt, False)
        return carry

    n_full = s0 >> 10
    lax.fori_loop(0, n_full, full_body, 0)
    tile(n_full, True)
    o_s_t = acc_ref[...] * (1.0 / l_ref[...])

    gt = gt_ref[0]
    outs = []
    for r in range(NSA_R):
        rs = slice(r * NSA_TQ, (r + 1) * NSA_TQ)
        outs.append(gt[:, 3 * r:3 * r + 1] * o_c[rs] + gt[:, 3 * r + 1:3 * r + 2] * o_s_t[:, rs].T
                    + gt[:, 3 * r + 2:3 * r + 3] * o_w[rs])
    o_ref[...] = jnp.concatenate(outs, axis=1).astype(o_ref.dtype)


def _nsa_prompt(z, gates_t, k_cmp, v_cmp, k_aug, vs_t, kvb, t, kw_col, vw_col):
    assert t % NSA_TK == 0 and t >= WINDOW + NSA_TQ and NSA_TQ == HEAD_DIM
    nc = k_cmp.shape[0]
    rows = NSA_R * NSA_TQ
    return pl.pallas_call(
        _nsa_prompt_kernel, grid=(NSA_KV, t // NSA_TQ),
        in_specs=[pl.BlockSpec((NSA_TQ, NSA_R * HEAD_DIM), lambda g, i: (i, g)),
                  pl.BlockSpec((1, NSA_TQ, 3 * NSA_R), lambda g, i: (g, i, 0)),
                  pl.BlockSpec((nc, HEAD_DIM), lambda g, i: (0, g)),
                  pl.BlockSpec((nc, HEAD_DIM), lambda g, i: (0, g)),
                  pl.BlockSpec((1, t, 2 * HEAD_DIM), lambda g, i: (g, 0, 0)),
                  pl.BlockSpec((1,) + vs_t.shape[1:], lambda g, i: (g, 0, 0, 0)),
                  pl.BlockSpec((t, HEAD_DIM), lambda g, i: (0, kw_col + g)),
                  pl.BlockSpec((t, HEAD_DIM), lambda g, i: (0, vw_col + g))],
        out_specs=pl.BlockSpec((NSA_TQ, NSA_R * HEAD_DIM), lambda g, i: (i, g)),
        out_shape=jax.ShapeDtypeStruct((t, NSA_HEADS * HEAD_DIM), BF16),
        scratch_shapes=[pltpu.VMEM((1, rows), F32), pltpu.VMEM((1, rows), F32),
                        pltpu.VMEM((HEAD_DIM, rows), F32)],
        compiler_params=_cparams(("arbitrary", "arbitrary")),
        name="nsa_prompt",
    )(z, gates_t, k_cmp, v_cmp, k_aug, vs_t, kvb, kvb)


SB_T = 256
SB_R = SB_HEADS // NSA_KV


def _strict_upper(n):
    return (_iota((n, n), 0) > _iota((n, n), 1)).astype(BF16)


def _log_sigmoids(z):
    lb = jnp.minimum(z, 0.0) - jnp.log(1.0 + jnp.exp(-jnp.abs(z)))
    return lb, lb - z


def _sb_tile(z, causal, carry, u):
    lb, lk = _log_sigmoids(z)
    if causal is not None:
        lk = jnp.where(causal, lk, 0.0)
    after = _dot(lk.astype(BF16), u) + carry
    a = jnp.exp(lb + after)
    if causal is not None:
        a = jnp.where(causal, a, 0.0)
    return a, carry + jnp.sum(lk, axis=1, keepdims=True)


def _sb_prompt_kernel(q_ref, k_ref, v_ref, o_ref, carry_ref, acc_ref):
    i = pl.program_id(1)
    rows = SB_R * SB_T
    q = q_ref[...]
    qr = jnp.concatenate([q[:, r * HEAD_DIM:(r + 1) * HEAD_DIM] for r in range(SB_R)], axis=0)
    u = _strict_upper(SB_T)
    s0 = pl.multiple_of(i * SB_T, SB_T)

    qi = jnp.concatenate([_iota((SB_T, 1), 0)] * SB_R, axis=0)
    causal = _iota((1, SB_T), 1) < qi
    z = _dot_nt(qr, k_ref[pl.ds(s0, SB_T), :])
    a, carry = _sb_tile(z, causal, jnp.zeros((rows, 1), F32), u)
    carry_ref[...] = carry
    acc_ref[...] = _dot(a.astype(BF16), v_ref[pl.ds(s0, SB_T), :])

    def body(kt, c):
        k0 = pl.multiple_of((i - 1 - kt) * SB_T, SB_T)
        zt = _dot_nt(qr, k_ref[pl.ds(k0, SB_T), :])
        at, cn = _sb_tile(zt, None, carry_ref[...], u)
        carry_ref[...] = cn
        acc_ref[...] += _dot(at.astype(BF16), v_ref[pl.ds(k0, SB_T), :])
        return c

    lax.fori_loop(0, i, body, 0)
    acc = acc_ref[...]
    o_ref[...] = jnp.concatenate([acc[r * SB_T:(r + 1) * SB_T] for r in range(SB_R)], axis=1).astype(o_ref.dtype)


def _sb_prompt(sq, sk, sv):
    t = sq.shape[0]
    assert t % SB_T == 0
    rows = SB_R * SB_T
    return pl.pallas_call(
        _sb_prompt_kernel, grid=(NSA_KV, t // SB_T),
        in_specs=[pl.BlockSpec((SB_T, SB_R * HEAD_DIM), lambda g, i: (i, g)),
                  pl.BlockSpec((t, HEAD_DIM), lambda g, i: (0, g)),
                  pl.BlockSpec((t, HEAD_DIM), lambda g, i: (0, g))],
        out_specs=pl.BlockSpec((SB_T, SB_R * HEAD_DIM), lambda g, i: (i, g)),
        out_shape=jax.ShapeDtypeStruct(sq.shape, BF16),
        scratch_shapes=[pltpu.VMEM((rows, 1), F32), pltpu.VMEM((rows, HEAD_DIM), F32)],
        compiler_params=_cparams(("arbitrary", "arbitrary")),
        name="sb_prompt",
    )(sq, sk, sv)


MEM_TQ = 512


def _mem_prompt_kernel(q_ref, k_ref, v_ref, o_ref):
    outs = []
    for h in range(MEM_HEADS):
        hs = slice(h * HEAD_DIM, (h + 1) * HEAD_DIM)
        s = _dot_nt(q_ref[:, hs].astype(BF16), k_ref[:, hs].astype(BF16)) * SCALE
        e = jnp.exp(s - jnp.max(s, axis=-1, keepdims=True))
        p = e * (1.0 / jnp.sum(e, axis=-1, keepdims=True))
        outs.append(_dot(p.astype(BF16), v_ref[:, hs].astype(BF16)))
    o_ref[...] = jnp.concatenate(outs, axis=1).astype(o_ref.dtype)


def _mem_prompt(z, mq_col, mk, mv, t):
    n_mem = mk.shape[0]
    w = MEM_HEADS * HEAD_DIM
    return pl.pallas_call(
        _mem_prompt_kernel, grid=(t // MEM_TQ,),
        in_specs=[pl.BlockSpec((MEM_TQ, w), lambda i: (i, mq_col)),
                  pl.BlockSpec((n_mem, w), lambda i: (0, 0)),
                  pl.BlockSpec((n_mem, w), lambda i: (0, 0))],
        out_specs=pl.BlockSpec((MEM_TQ, w), lambda i: (i, 0)),
        out_shape=jax.ShapeDtypeStruct((t, w), BF16),
        compiler_params=_cparams(("arbitrary",)),
        name="mem_prompt",
    )(z, mk, mv)


HROWS = 16


def _rows_of_group(n_rows, g, heads_per_group):
    row = _iota((n_rows, 1), 0)
    return (row >= g * heads_per_group) & (row < (g + 1) * heads_per_group)


def _group_lanes(full, heads_per_group):
    out = jnp.zeros((full.shape[0], HEAD_DIM), F32)
    for g in range(NSA_KV):
        out = out + jnp.where(_rows_of_group(full.shape[0], g, heads_per_group),
                              full[:, g * HEAD_DIM:(g + 1) * HEAD_DIM], 0.0)
    return out


def _token_rows(ref, n_tok, lead=()):
    return jnp.concatenate([ref[(*lead, pl.ds(g, n_tok, stride=NSA_KV), slice(None))] for g in range(NSA_KV)],
                           axis=1).astype(BF16)


def _page_specs(n_pages):
    return [pl.BlockSpec((PAGE * NSA_KV, HEAD_DIM), lambda b, pt, c=c: (pt[b * n_pages + c], 0))
            for c in range(n_pages)]


def _dec_sb_kernel(pt_ref, q_ref, *refs, n_pages):
    del pt_ref
    k_refs, v_refs, o_ref = refs[:n_pages], refs[n_pages:2 * n_pages], refs[2 * n_pages]
    q = q_ref[0]
    u = _strict_upper(PAGE)
    z = jnp.concatenate([_dot_nt(q, _token_rows(r, PAGE)) for r in k_refs], axis=1)
    lb, lk = _log_sigmoids(z)
    lkb = lk.astype(BF16)
    page = lambda x, c: x[:, c * PAGE:(c + 1) * PAGE]
    carry = jnp.zeros((HROWS, 1), F32)
    acc = jnp.zeros((HROWS, KVW), F32)
    for c in reversed(range(n_pages)):
        after = _dot(page(lkb, c), u) + carry
        a = jnp.exp(page(lb, c) + after)
        acc = acc + _dot(a.astype(BF16), _token_rows(v_refs[c], PAGE))
        carry = carry + jnp.sum(page(lk, c), axis=1, keepdims=True)
    o_ref[0] = _group_lanes(acc, SB_R).astype(o_ref.dtype)


def _dec_sb(page_ids, q_rows, cache_k, cache_v, n_pages):
    nb = q_rows.shape[0]
    grid_spec = pltpu.PrefetchScalarGridSpec(
        num_scalar_prefetch=1, grid=(nb,),
        in_specs=[pl.BlockSpec((1, HROWS, KVW), lambda b, pt: (b, 0, 0))] + 2 * _page_specs(n_pages),
        out_specs=pl.BlockSpec((1, HROWS, HEAD_DIM), lambda b, pt: (b, 0, 0)))
    return pl.pallas_call(
        functools.partial(_dec_sb_kernel, n_pages=n_pages), grid_spec=grid_spec,
        out_shape=jax.ShapeDtypeStruct((nb, HROWS, HEAD_DIM), BF16),
        compiler_params=_cparams(("arbitrary",)),
        name="dec_sb",
    )(page_ids, q_rows, *([cache_k] * n_pages), *([cache_v] * n_pages))


def _bf16_round(x):
    return x.astype(BF16).astype(F32)


def _pad_halves(x, half_rows):
    n = x.shape[0] // 2
    zeros = jnp.zeros((half_rows - n, x.shape[1]), x.dtype)
    return jnp.concatenate([x[:n], zeros, x[n:], zeros], axis=0)


def _dec_nsa_kernel(pt_ref, q_ref, gt_ref, kc_ref, vc_ref, wk_ref, wv_ref, nks_ref, nvs_ref, nkw_ref, nvw_ref,
                    *refs, past, n_pages):
    del pt_ref
    ks_refs, vs_refs, o_ref = refs[:n_pages], refs[n_pages:2 * n_pages], refs[2 * n_pages]
    q = q_ref[0]
    qf = q.astype(F32)

    nb = HEAD_DIM
    kc = _pad_halves(kc_ref[0].astype(BF16), nb)
    vc = _pad_halves(vc_ref[0].astype(BF16), nb)
    sc = _dot_nt(q, kc) * SCALE
    pc = _softmax_masked(sc, _cmp_block_end(2 * nb) <= past)
    o_c = _group_lanes(_dot(pc.astype(BF16), vc), NSA_R)
    imp = jnp.zeros((8, nb), F32)
    for g in range(NSA_KV):
        pcs = jnp.sum(pc[g * NSA_R:(g + 1) * NSA_R], axis=0, keepdims=True)
        imp = imp + jnp.where(_iota((8, 1), 0) == g, pcs[:, :nb] + pcs[:, nb:], 0.0)
    sel = _select_mask(imp, jnp.full((1, 1), past, jnp.int32)).astype(F32)
    pk = jnp.zeros((HROWS, nb), F32)
    for g in range(NSA_KV):
        pk = pk + jnp.where(_rows_of_group(HROWS, g, NSA_R), sel[g:g + 1, :], 0.0)

    s = jnp.concatenate([_dot_nt(q, _token_rows(r, PAGE)) for r in ks_refs], axis=1) * SCALE
    expand = (_iota((nb, past), 0) == (_iota((nb, past), 1) >> 6)).astype(BF16)
    s = jnp.where(_dot(pk.astype(BF16), expand) > 0.5, s, NEG)
    s_new = jnp.sum(qf * _bf16_round(nks_ref[0]), axis=1, keepdims=True) * SCALE
    nblk = past // SEL_BLOCK
    s_new = jnp.where(pk[:, nblk:nblk + 1] > 0.5, s_new, NEG)
    ms = jnp.maximum(jnp.maximum(jnp.max(s, axis=1, keepdims=True), s_new), M_FLOOR)
    e = jnp.exp(s - ms)
    e_new = jnp.exp(s_new - ms)
    ls = jnp.sum(e, axis=1, keepdims=True) + e_new
    acc = _bf16_round(e_new) * _bf16_round(nvs_ref[0])
    for c in range(n_pages):
        acc = acc + _dot(e[:, c * PAGE:(c + 1) * PAGE].astype(BF16), _token_rows(vs_refs[c], PAGE))
    o_s = _group_lanes(acc, NSA_R) * (1.0 / ls)

    wl = wk_ref.shape[1] // NSA_KV
    sw = _dot_nt(q, _token_rows(wk_ref, wl, (0,))) * SCALE
    wmask = past - wl + _iota((1, wl), 1) > past - WINDOW
    sw = jnp.where(wmask, sw, NEG)
    sw_new = jnp.sum(qf * _bf16_round(nkw_ref[0]), axis=1, keepdims=True) * SCALE
    mw = jnp.maximum(jnp.max(sw, axis=1, keepdims=True), sw_new)
    ew = jnp.where(wmask, jnp.exp(sw - mw), 0.0)
    ew_new = jnp.exp(sw_new - mw)
    lw = jnp.sum(ew, axis=1, keepdims=True) + ew_new
    accw = _dot(ew.astype(BF16), _token_rows(wv_ref, wl, (0,))) + _bf16_round(ew_new) * _bf16_round(nvw_ref[0])
    o_w = _group_lanes(accw, NSA_R) * (1.0 / lw)

    gt = gt_ref[0]
    o_ref[0] = (gt[:, 0:1] * o_c + gt[:, 1:2] * o_s + gt[:, 2:3] * o_w).astype(o_ref.dtype)


def _dec_nsa(page_ids, q_rows, gates, k_cmp, v_cmp, cache_k, cache_v, win_k, win_v, new_ks, new_vs, new_kw,
             new_vw, n_pages):
    nb = q_rows.shape[0]
    nc = k_cmp.shape[1]
    past = n_pages * PAGE
    assert nc <= 2 * HEAD_DIM and past // SEL_BLOCK < HEAD_DIM
    per_b = lambda shape: pl.BlockSpec((1,) + shape, lambda b, pt: (b, 0, 0))
    grid_spec = pltpu.PrefetchScalarGridSpec(
        num_scalar_prefetch=1, grid=(nb,),
        in_specs=[per_b((HROWS, KVW)), per_b((HROWS, 3)), per_b((nc, KVW)), per_b((nc, KVW)),
                  per_b(win_k.shape[1:]), per_b(win_v.shape[1:]),
                  per_b((1, KVW)), per_b((1, KVW)), per_b((1, KVW)), per_b((1, KVW))] + 2 * _page_specs(n_pages),
        out_specs=per_b((HROWS, HEAD_DIM)))
    return pl.pallas_call(
        functools.partial(_dec_nsa_kernel, past=past, n_pages=n_pages), grid_spec=grid_spec,
        out_shape=jax.ShapeDtypeStruct((nb, HROWS, HEAD_DIM), BF16),
        compiler_params=_cparams(("arbitrary",)),
        name="dec_nsa",
    )(page_ids, q_rows, gates, k_cmp, v_cmp, win_k, win_v, new_ks, new_vs, new_kw, new_vw,
      *([cache_k] * n_pages), *([cache_v] * n_pages))


def _dec_mem_kernel(q_ref, k_ref, v_ref, o_ref):
    n_mem = k_ref.shape[1] // NSA_KV
    s = _dot_nt(q_ref[0], _token_rows(k_ref, n_mem, (0,))) * SCALE
    e = jnp.exp(s - jnp.max(s, axis=-1, keepdims=True))
    p = e * (1.0 / jnp.sum(e, axis=-1, keepdims=True))
    o_ref[0] = _group_lanes(_dot(p.astype(BF16), _token_rows(v_ref, n_mem, (0,))), 1).astype(o_ref.dtype)


def _dec_mem(q_rows, mem_k, mem_v):
    nb = mem_k.shape[0]
    rows = q_rows.shape[1]
    per_b = lambda shape: pl.BlockSpec((1,) + shape, lambda b: (b, 0, 0))
    return pl.pallas_call(
        _dec_mem_kernel, grid=(nb,),
        in_specs=[per_b((rows, KVW)), per_b(mem_k.shape[1:]), per_b(mem_v.shape[1:])],
        out_specs=per_b((rows, HEAD_DIM)),
        out_shape=jax.ShapeDtypeStruct((nb, rows, HEAD_DIM), BF16),
        compiler_params=_cparams(("arbitrary",)),
        name="dec_mem",
    )(q_rows, mem_k, mem_v)


def _head_rows(q, n_heads, heads_per_group, rows):
    nb = q.shape[0]
    qh = q.reshape(nb, n_heads, 1, HEAD_DIM)
    onehot = (jnp.arange(n_heads)[:, None] // heads_per_group == jnp.arange(NSA_KV)[None, :])
    full = jnp.where(onehot[None, :, :, None], qh, 0.0).reshape(nb, n_heads, KVW)
    return jnp.pad(full, ((0, 0), (0, rows - n_heads), (0, 0))).astype(BF16)


def _even_odd(x, lanes_half):
    n = x.shape[-2]
    pad = [(0, 0)] * (x.ndim - 2) + [(0, lanes_half - n // 2), (0, 0)]
    return jnp.concatenate([jnp.pad(x[..., 0::2, :], pad), jnp.pad(x[..., 1::2, :], pad)], axis=-2)


def _rope_tables(pos):
    half = ROPE_DIMS // 2
    inv = jnp.power(ROPE_THETA, -jnp.arange(half, dtype=F32) * 2.0 / ROPE_DIMS)
    ang = pos.astype(F32)[:, None] * inv[None, :]
    cos, sin = jnp.cos(ang), jnp.sin(ang)
    n = pos.shape[0]
    rest = HEAD_DIM - ROPE_DIMS
    c = jnp.concatenate([cos, cos, jnp.ones((n, rest), F32)], axis=1)
    s_lo = jnp.concatenate([-sin, jnp.zeros((n, HEAD_DIM - half), F32)], axis=1)
    s_hi = jnp.concatenate([jnp.zeros((n, half), F32), sin, jnp.zeros((n, rest), F32)], axis=1)
    return c, s_lo, s_hi


C_Q, C_KC, C_VC, C_KS, C_VS, C_KW, C_VW = 0, 2048, 2560, 3072, 3584, 4096, 4608
C_SQ, C_SK, C_SV, C_MQ, C_BG, C_END = 5120, 6656, 7168, 7680, 8192, 20480
NG_COLS = NSA_HEADS * 3
NG_SRC = 5120


def kernel(x_prompt, mem_prompt, x_sample, cache_cmp_k, cache_cmp_v, cache_sel_k, cache_sel_v, cache_sb_k,
           cache_sb_v, state_win_k, state_win_v, cache_mem_k, cache_mem_v, page_table, norm1_w, w_in, g_nsa_q,
           g_cmp_k, g_sel_k, g_win_k, g_mem_q, g_mem_k, mem_norm_w, w_mem_kv, cmp_pos_k, cmp_k_w1, cmp_k_w2,
           cmp_pos_v, cmp_v_w1, cmp_v_w2, w_up_nsa, w_up_sb, w_up_mem, w_o, norm2_w, w_ff1, w_ff2):
    tp = x_prompt.shape[1]
    nb = x_sample.shape[0]
    n_pages = page_table.shape[1]
    past = n_pages * PAGE
    m = tp + nb
    tm = _row_tile(m, 832)
    assert x_prompt.shape[0] == 1 and x_sample.shape[1] == 1

    h = _rms_stack(x_prompt[0], x_sample[:, 0], norm1_w)
    w_a = w_in[:, :NG_SRC].astype(BF16)
    w_b = w_in[:, NG_SRC + NG_COLS:].astype(BF16)
    w_ng = jnp.pad(w_in[:, NG_SRC:NG_SRC + NG_COLS], ((0, 0), (0, HEAD_DIM - NG_COLS))).astype(BF16)
    pos = jnp.concatenate([jnp.arange(tp, dtype=jnp.int32), jnp.full((nb,), past, jnp.int32)])
    rope_c, rope_lo, rope_hi = _rope_tables(pos)
    modes = np.zeros((C_END // IN_TN,), np.int32)
    gidx = np.zeros((C_END // IN_TN,), np.int32)
    for c0, c1, md, gi in ((C_Q, C_KC, MODE_NORM_ROPE, 0), (C_KC, C_VC, MODE_NORM_ROPE, 1),
                           (C_KS, C_VS, MODE_NORM_ROPE, 2), (C_KW, C_VW, MODE_NORM_ROPE, 3),
                           (C_MQ, C_BG, MODE_NORM, 4)):
        modes[c0 // IN_TN:c1 // IN_TN] = md
        gidx[c0 // IN_TN:c1 // IN_TN] = gi
    gains = jnp.stack([g_nsa_q, g_cmp_k, g_sel_k, g_win_k, g_mem_q]).reshape(5, 1, HEAD_DIM)
    na = C_SQ // IN_TN
    z_a = _inproj(h, w_a, jnp.asarray(modes[:na]), jnp.asarray(gidx[:na]), gains, rope_c, rope_lo, rope_hi, tm)
    z_b = _inproj(h, w_b, jnp.asarray(modes[na:]), jnp.asarray(gidx[na:]), gains, rope_c, rope_lo, rope_hi, tm)
    ng = _mm(h, w_ng, tm=tm, tn=HEAD_DIM, tk=D_MODEL, out_dtype=F32, epi=_epi_sigmoid, name="head_gates")

    def zcols(c0, c1, r0, r1):
        arr, off = (z_a, 0) if c0 < C_SQ else (z_b, C_SQ)
        return arr[r0:r1, c0 - off:c1 - off]

    zp = lambda c0, c1: zcols(c0, c1, 0, tp)
    zs = lambda c0, c1: zcols(c0, c1, tp, m)
    kv_out = lambda a: a.reshape(a.shape[0], NSA_KV, HEAD_DIM)
    kv_cols = (C_KC, C_VC, C_KS, C_VS, C_SK, C_SV)
    p_kv = [kv_out(zp(c, c + KVW))[None] for c in kv_cols]
    s_kv = [kv_out(zs(c, c + KVW))[:, None] for c in kv_cols]
    keep = min(WINDOW, tp)
    p_win = [kv_out(zcols(c, c + KVW, tp - keep, tp))[None] for c in (C_KW, C_VW)]
    new_win = [kv_out(zs(c, c + KVW))[:, None] for c in (C_KW, C_VW)]
    lb = state_win_k.shape[1]
    keep_s = min(WINDOW, past + 1)
    s_win = [jnp.concatenate([st, nw], axis=1)[:, lb + 1 - keep_s:]
             for st, nw in ((state_win_k, new_win[0]), (state_win_v, new_win[1]))]

    w1k, w2k = cmp_k_w1.astype(BF16), cmp_k_w2.astype(BF16)
    w1v, w2v = cmp_v_w1.astype(BF16), cmp_v_w2.astype(BF16)
    ids_p = jnp.arange(tp // PAGE, dtype=jnp.int32)
    n_cmp = tp // CMP_BLOCK
    token_rows = lambda a: a.reshape(-1, HEAD_DIM)
    k_cmp_p = _even_odd(_compress(token_rows(p_kv[0]), ids_p, cmp_pos_k, w1k, w2k).reshape(n_cmp, KVW), HEAD_DIM)
    v_cmp_p = _even_odd(_compress(token_rows(p_kv[1]), ids_p, cmp_pos_v, w1v, w2v).reshape(n_cmp, KVW), HEAD_DIM)

    gates_all = ng[:, :NG_COLS].reshape(m, NSA_KV, 3 * NSA_R)
    gates_t = jnp.transpose(gates_all[:tp], (1, 0, 2))
    kvb = zp(C_KC, C_SQ).astype(BF16)
    blk_of_key = jnp.arange(tp, dtype=jnp.int32) // SEL_BLOCK
    onehot = (blk_of_key[:, None] == jnp.arange(HEAD_DIM, dtype=jnp.int32)[None, :]).astype(BF16)
    ks_b = jnp.transpose(zp(C_KS, C_VS).astype(BF16).reshape(tp, NSA_KV, HEAD_DIM), (1, 0, 2))
    k_aug = jnp.concatenate([ks_b, jnp.broadcast_to(onehot[None], ks_b.shape)], axis=2)
    vs_t = jnp.transpose(zp(C_VS, C_KW).astype(BF16).reshape(tp // NSA_TK, NSA_TK, NSA_KV, HEAD_DIM), (2, 0, 3, 1))
    cb = lambda c0: (c0 - C_KC) // HEAD_DIM
    o_nsa_p = _nsa_prompt(z_a, gates_t, k_cmp_p, v_cmp_p, k_aug, vs_t, kvb, tp, cb(C_KW), cb(C_VW))

    o_sb_p = _sb_prompt((zp(C_SQ, C_SK) * SCALE).astype(BF16), zp(C_SK, C_SV).astype(BF16),
                        zp(C_SV, C_MQ).astype(BF16))

    mem_h = _rms(mem_prompt[0], mem_norm_w, mem_prompt.shape[1])
    w_mkv = w_mem_kv.astype(BF16)
    n_mem = mem_prompt.shape[1]
    mw = MEM_HEADS * HEAD_DIM
    mk = _mm(mem_h, w_mkv, tm=n_mem, tn=mw, tk=D_MODEL, out_dtype=F32, epi=_epi_headnorm,
             extras=((g_mem_k.reshape(1, HEAD_DIM), 0),), ncols=mw, name="mem_k")
    mv = _mm(mem_h, w_mkv, tm=n_mem, tn=mw, tk=D_MODEL, out_dtype=F32, col0=1, name="mem_v")
    o_mem_p = _mem_prompt(z_b, (C_MQ - C_SQ) // mw, mk, mv, tp)

    pt = page_table.reshape(-1).astype(jnp.int32)
    n_cmp_s = past // CMP_BLOCK
    k_cmp_s = _even_odd(_compress(token_rows(cache_cmp_k), pt, cmp_pos_k, w1k, w2k), n_cmp_s // 2)
    v_cmp_s = _even_odd(_compress(token_rows(cache_cmp_v), pt, cmp_pos_v, w1v, w2v), n_cmp_s // 2)
    q_nsa_s = _head_rows(zs(C_Q, C_KC), NSA_HEADS, NSA_R, HROWS)
    gates_s = ng[tp:, :NG_COLS].reshape(nb, NSA_HEADS, 3)
    seq_rows = lambda a: a.reshape(a.shape[0], -1, HEAD_DIM)
    new_row = lambda c0: zs(c0, c0 + KVW).reshape(nb, 1, KVW)
    o_nsa_s = _dec_nsa(pt, q_nsa_s, gates_s, k_cmp_s, v_cmp_s, token_rows(cache_sel_k), token_rows(cache_sel_v),
                       seq_rows(state_win_k), seq_rows(state_win_v), new_row(C_KS), new_row(C_VS), new_row(C_KW),
                       new_row(C_VW), n_pages)
    q_sb_s = _head_rows(zs(C_SQ, C_SK) * SCALE, SB_HEADS, SB_R, HROWS)
    o_sb_s = _dec_sb(pt, q_sb_s, token_rows(cache_sb_k), token_rows(cache_sb_v), n_pages)
    q_mem_s = _head_rows(zs(C_MQ, C_BG), MEM_HEADS, 1, 8)
    o_mem_s = _dec_mem(q_mem_s, seq_rows(cache_mem_k), seq_rows(cache_mem_v))

    o_nsa = jnp.concatenate([o_nsa_p, o_nsa_s.reshape(nb, NSA_HEADS * HEAD_DIM)], axis=0)
    o_sb = jnp.concatenate([o_sb_p, o_sb_s[:, :SB_HEADS].reshape(nb, SB_HEADS * HEAD_DIM)], axis=0)
    o_mem = jnp.concatenate([o_mem_p, o_mem_s[:, :MEM_HEADS].reshape(nb, mw)], axis=0)

    u = _merge_up(o_nsa, o_sb, o_mem, w_up_nsa.astype(BF16), w_up_sb.astype(BF16), w_up_mem.astype(BF16),
                  z_b, C_BG - C_SQ, tm, 512)
    x_all = jnp.concatenate([x_prompt[0], x_sample[:, 0]], axis=0)
    x1 = _mm(u, w_o.astype(BF16), tm=tm, tn=1024, tk=D_MODEL, out_dtype=F32, epi=_epi_residual,
             extras=((x_all, 0),), name="out_proj")
    h2 = _rms(x1, norm2_w, _row_tile(m, 512))
    a = _mm(h2, w_ff1.astype(BF16), tm=tm, tn=1024, tk=D_MODEL, out_dtype=BF16, epi=_epi_relu2, name="ff1")
    y = _mm(a, w_ff2.astype(BF16), tm=tm, tn=1024, tk=4096, out_dtype=F32, epi=_epi_residual,
            extras=((x1, 0),), name="ff2")

    y_prompt = y[:tp][None]
    y_sample = y[tp:][:, None]
    p_mem = [mk.reshape(1, n_mem, MEM_HEADS, HEAD_DIM), mv.reshape(1, n_mem, MEM_HEADS, HEAD_DIM)]
    return (y_prompt, y_sample, *p_kv, *p_win, *p_mem, *s_kv, *s_win)
```

```python
import functools
import math

import numpy as np
import jax
import jax.numpy as jnp
from jax import lax
from jax.experimental import pallas as pl
from jax.experimental.pallas import tpu as pltpu

F32 = jnp.float32
BF16 = jnp.bfloat16

D_MODEL = 4096
HEAD_DIM = 128
NSA_HEADS = 16
NSA_KV = 4
SB_HEADS = 12
MEM_HEADS = 4
CMP_BLOCK = 32
SEL_BLOCK = 64
N_SEL = 16
WINDOW = 512
PAGE = 128
ROPE_THETA = 500000.0
ROPE_DIMS = HEAD_DIM // 4
EPS = 1e-6
NEG = -1e30
M_FLOOR = -1e20
FORCE_BONUS = 1e4
SCALE = HEAD_DIM ** -0.5
KVW = NSA_KV * HEAD_DIM

VMEM_LIMIT = 56 * 1024 * 1024


def _row_tile(m, cap):
    for step in (16, 8):
        best = max((t for t in range(step, cap + 1, step) if m % t == 0), default=0)
        if best:
            return best
    raise ValueError(f"no row tile for {m}")


def _cparams(sem, vmem=VMEM_LIMIT):
    return pltpu.CompilerParams(dimension_semantics=sem, vmem_limit_bytes=vmem)


def _dot(a, b):
    return jnp.dot(a, b, preferred_element_type=F32)


def _dot_nt(a, b):
    return lax.dot_general(a, b, (((1,), (1,)), ((), ())), preferred_element_type=F32)


def _iota(shape, dim):
    return lax.broadcasted_iota(jnp.int32, shape, dim)


def _rms2_kernel(xp_ref, xs_ref, w_ref, o_ref, *, n_prompt_tiles):
    i = pl.program_id(0)
    x = jnp.where(i < n_prompt_tiles, xp_ref[...], xs_ref[...])
    ms = jnp.mean(x * x, axis=-1, keepdims=True)
    o_ref[...] = (x * lax.rsqrt(ms + EPS) * w_ref[...]).astype(o_ref.dtype)


def _rms_stack(xp, xs, w):
    tp, d = xp.shape
    ts = xs.shape[0]
    assert tp % ts == 0 and ts % 8 == 0
    npt = tp // ts
    return pl.pallas_call(
        functools.partial(_rms2_kernel, n_prompt_tiles=npt),
        grid=(npt + 1,),
        in_specs=[pl.BlockSpec((ts, d), lambda i: (jnp.minimum(i, npt - 1), 0)),
                  pl.BlockSpec((ts, d), lambda i: (0, 0)),
                  pl.BlockSpec((1, d), lambda i: (0, 0))],
        out_specs=pl.BlockSpec((ts, d), lambda i: (i, 0)),
        out_shape=jax.ShapeDtypeStruct((tp + ts, d), BF16),
        compiler_params=_cparams(("arbitrary",)),
        name="rms_stack",
    )(xp, xs, w.reshape(1, d))


def _rms_kernel(x_ref, w_ref, o_ref):
    x = x_ref[...]
    ms = jnp.mean(x * x, axis=-1, keepdims=True)
    o_ref[...] = (x * lax.rsqrt(ms + EPS) * w_ref[...]).astype(o_ref.dtype)


def _rms(x, w, tm):
    m, d = x.shape
    return pl.pallas_call(
        _rms_kernel, grid=(m // tm,),
        in_specs=[pl.BlockSpec((tm, d), lambda i: (i, 0)), pl.BlockSpec((1, d), lambda i: (0, 0))],
        out_specs=pl.BlockSpec((tm, d), lambda i: (i, 0)),
        out_shape=jax.ShapeDtypeStruct((m, d), BF16),
        compiler_params=_cparams(("arbitrary",)),
        name="rms",
    )(x, w.reshape(1, d))


MODE_PLAIN, MODE_NORM_ROPE, MODE_NORM = 0, 1, 2
IN_TN = 512
IN_PARTS = 2


def _head_norm(z, g):
    outs = []
    for hh in range(z.shape[1] // HEAD_DIM):
        xs = z[:, hh * HEAD_DIM:(hh + 1) * HEAD_DIM]
        ms = jnp.mean(xs * xs, axis=-1, keepdims=True)
        outs.append(xs * lax.rsqrt(ms + EPS) * g)
    return outs


def _rope(x, c, s_lo, s_hi):
    half = ROPE_DIMS // 2
    return x * c + pltpu.roll(x, HEAD_DIM - half, 1) * s_lo + pltpu.roll(x, half, 1) * s_hi


def _inproj_kernel(mode_ref, gidx_ref, a_ref, b_ref, g_ref, c_ref, slo_ref, shi_ref, o_ref):
    j = pl.program_id(1)
    o_ref[...] = _dot(a_ref[...], b_ref[...])
    for part in range(IN_PARTS):
        cols = slice(part * IN_TN, (part + 1) * IN_TN)
        mode = mode_ref[j * IN_PARTS + part]
        gain = g_ref[gidx_ref[j * IN_PARTS + part]]

        @pl.when(mode == MODE_NORM)
        def _():
            o_ref[:, cols] = jnp.concatenate(_head_norm(o_ref[:, cols], gain), axis=1)

        @pl.when(mode == MODE_NORM_ROPE)
        def _():
            c, slo, shi = c_ref[...], slo_ref[...], shi_ref[...]
            o_ref[:, cols] = jnp.concatenate([_rope(y, c, slo, shi) for y in _head_norm(o_ref[:, cols], gain)],
                                             axis=1)


def _inproj(h, w, modes, gidx, gains, rope_c, rope_lo, rope_hi, tm):
    m, k = h.shape
    n = w.shape[1]
    tn = IN_PARTS * IN_TN
    grid_spec = pltpu.PrefetchScalarGridSpec(
        num_scalar_prefetch=2, grid=(m // tm, n // tn),
        in_specs=[pl.BlockSpec((tm, k), lambda i, j, *_: (i, 0)),
                  pl.BlockSpec((k, tn), lambda i, j, *_: (0, j)),
                  pl.BlockSpec(gains.shape, lambda i, j, *_: (0, 0, 0)),
                  pl.BlockSpec((tm, HEAD_DIM), lambda i, j, *_: (i, 0)),
                  pl.BlockSpec((tm, HEAD_DIM), lambda i, j, *_: (i, 0)),
                  pl.BlockSpec((tm, HEAD_DIM), lambda i, j, *_: (i, 0))],
        out_specs=pl.BlockSpec((tm, tn), lambda i, j, *_: (i, j)))
    return pl.pallas_call(
        _inproj_kernel, grid_spec=grid_spec,
        out_shape=jax.ShapeDtypeStruct((m, n), F32),
        compiler_params=_cparams(("arbitrary", "arbitrary")),
        name="inproj",
    )(modes, gidx, h, w, gains, rope_c, rope_lo, rope_hi)


def _mm_kernel(*refs, nk, n_extra, epi):
    a_ref, b_ref = refs[0], refs[1]
    extra = refs[2:2 + n_extra]
    o_ref = refs[2 + n_extra]
    part = _dot(a_ref[...], b_ref[...])
    if nk == 1:
        o_ref[...] = epi(part, *[e[...] for e in extra]).astype(o_ref.dtype)
    else:
        acc_ref = refs[3 + n_extra]
        kk = pl.program_id(2)

        @pl.when(kk == 0)
        def _():
            acc_ref[...] = part

        @pl.when(kk > 0)
        def _():
            acc_ref[...] += part

        @pl.when(kk == nk - 1)
        def _():
            o_ref[...] = epi(acc_ref[...], *[e[...] for e in extra]).astype(o_ref.dtype)


def _mm(a, b, *, tm, tn, tk, out_dtype, epi=None, extras=(), col0=0, ncols=None, name="mm"):
    m, k = a.shape
    ncols = b.shape[1] - col0 * tn if ncols is None else ncols
    assert m % tm == 0 and ncols % tn == 0 and k % tk == 0
    nk = k // tk
    if epi is None:
        epi = lambda z: z
    in_specs = [pl.BlockSpec((tm, tk), lambda i, j, kk: (i, kk)),
                pl.BlockSpec((tk, tn), lambda i, j, kk: (kk, j + col0))]
    args = [a, b]
    for arr, off in extras:
        if arr.shape[0] == 1:
            in_specs.append(pl.BlockSpec((1, arr.shape[1]), lambda i, j, kk: (0, 0)))
        else:
            in_specs.append(pl.BlockSpec((tm, tn), lambda i, j, kk, off=off: (i, j + off)))
        args.append(arr)
    scratch = [pltpu.VMEM((tm, tn), F32)] if nk > 1 else []
    return pl.pallas_call(
        functools.partial(_mm_kernel, nk=nk, n_extra=len(extras), epi=epi),
        grid=(m // tm, ncols // tn, nk),
        in_specs=in_specs,
        out_specs=pl.BlockSpec((tm, tn), lambda i, j, kk: (i, j)),
        out_shape=jax.ShapeDtypeStruct((m, ncols), out_dtype),
        scratch_shapes=scratch,
        compiler_params=_cparams(("arbitrary", "arbitrary", "arbitrary")),
        name=name,
    )(*args)


def _epi_headnorm(z, g):
    return jnp.concatenate(_head_norm(z, g), axis=1)


def _epi_sigmoid(z):
    return jax.nn.sigmoid(z)


def _epi_residual(z, x):
    return x + z


def _epi_relu2(z):
    return jnp.square(jnp.maximum(z, 0.0))


def _up_kernel(on_ref, os_ref, om_ref, wn_ref, ws_ref, wm_ref, g0_ref, g1_ref, g2_ref, o_ref):
    sig = jax.nn.sigmoid
    u = (sig(g0_ref[...]) * _dot(on_ref[...], wn_ref[...])
         + sig(g1_ref[...]) * _dot(os_ref[...], ws_ref[...])
         + sig(g2_ref[...]) * _dot(om_ref[...], wm_ref[...]))
    o_ref[...] = u.astype(o_ref.dtype)


def _merge_up(o_nsa, o_sb, o_mem, w_n, w_s, w_m, z, gate_col0, tm, tn):
    m = o_nsa.shape[0]
    d = w_n.shape[1]
    gb = gate_col0 // tn
    nd = d // tn
    row = lambda kdim: pl.BlockSpec((tm, kdim), lambda i, j: (i, 0))
    col = lambda kdim: pl.BlockSpec((kdim, tn), lambda i, j: (0, j))
    gate = lambda c: pl.BlockSpec((tm, tn), lambda i, j, c=c: (i, gb + c * nd + j))
    return pl.pallas_call(
        _up_kernel, grid=(m // tm, nd),
        in_specs=[row(o_nsa.shape[1]), row(o_sb.shape[1]), row(o_mem.shape[1]),
                  col(w_n.shape[0]), col(w_s.shape[0]), col(w_m.shape[0]),
                  gate(0), gate(1), gate(2)],
        out_specs=pl.BlockSpec((tm, tn), lambda i, j: (i, j)),
        out_shape=jax.ShapeDtypeStruct((m, d), BF16),
        compiler_params=_cparams(("arbitrary", "arbitrary")),
        name="merge_up",
    )(o_nsa, o_sb, o_mem, w_n, w_s, w_m, z, z, z)


CMP_PAGES = 16
CMP_HID = 256
CMP_PITCH = 136


def _gelu_tanh(x):
    cdf = 0.5 * (1.0 + jnp.tanh(math.sqrt(2.0 / math.pi) * (x + 0.044715 * (x * x * x))))
    return x * cdf


def _compress_kernel(pt_ref, *refs):
    del pt_ref
    page_refs = refs[:CMP_PAGES]
    pos_ref, w1_ref, w2_ref, o_ref, buf_ref = refs[CMP_PAGES:]
    brow = CMP_BLOCK * NSA_KV
    for c in range(CMP_PAGES):
        for n in range(PAGE // CMP_BLOCK):
            blk = c * (PAGE // CMP_BLOCK) + n
            buf_ref[blk * CMP_PITCH:blk * CMP_PITCH + brow, :] = page_refs[c][n * brow:(n + 1) * brow, :]
    nrow = CMP_PAGES * (PAGE // CMP_BLOCK)
    acc = jnp.zeros((NSA_KV * nrow, CMP_HID), F32)
    def token(t):
        parts = [buf_ref[pl.ds(t * NSA_KV + g, nrow, stride=CMP_PITCH), :] for g in range(NSA_KV)]
        return (jnp.concatenate(parts, axis=0) + pos_ref[t:t + 1, :]).astype(BF16)

    accs = [acc, acc]
    for i, t in enumerate(range(0, CMP_BLOCK, 2)):
        x = jnp.concatenate([token(t), token(t + 1)], axis=1)
        accs[i % 2] = accs[i % 2] + _dot(x, w1_ref[t * HEAD_DIM:(t + 2) * HEAD_DIM, :])
    acc = accs[0] + accs[1]
    out = _dot(_gelu_tanh(acc).astype(BF16), w2_ref[...])
    for g in range(NSA_KV):
        o_ref[0, :, g * HEAD_DIM:(g + 1) * HEAD_DIM] = out[g * nrow:(g + 1) * nrow, :]


def _compress(pages, page_ids, pos, w1, w2):
    n_steps = page_ids.shape[0] // CMP_PAGES
    nrow = CMP_PAGES * (PAGE // CMP_BLOCK)
    page_specs = [pl.BlockSpec((PAGE * NSA_KV, HEAD_DIM), lambda s, pt, c=c: (pt[s * CMP_PAGES + c], 0))
                  for c in range(CMP_PAGES)]
    grid_spec = pltpu.PrefetchScalarGridSpec(
        num_scalar_prefetch=1, grid=(n_steps,),
        in_specs=page_specs + [pl.BlockSpec(pos.shape, lambda s, pt: (0, 0)),
                               pl.BlockSpec(w1.shape, lambda s, pt: (0, 0)),
                               pl.BlockSpec(w2.shape, lambda s, pt: (0, 0))],
        out_specs=pl.BlockSpec((1, nrow, KVW), lambda s, pt: (s, 0, 0)),
        scratch_shapes=[pltpu.VMEM((nrow * CMP_PITCH, HEAD_DIM), F32)])
    return pl.pallas_call(
        _compress_kernel, grid_spec=grid_spec,
        out_shape=jax.ShapeDtypeStruct((n_steps, nrow, KVW), F32),
        compiler_params=_cparams(("arbitrary",)),
        name="compress",
    )(page_ids, *([pages] * CMP_PAGES), pos, w1, w2)


def _softmax_masked(s2, mask):
    sm = jnp.where(mask, s2, NEG)
    e = jnp.exp2(sm - jnp.max(sm, axis=-1, keepdims=True))
    p = e * (1.0 / jnp.sum(e, axis=-1, keepdims=True))
    return jnp.where(mask, p, 0.0)


def _cmp_block_end(n_lanes):
    half = n_lanes // 2
    lane = _iota((1, n_lanes), 1)
    blk = jnp.where(lane < half, 2 * lane, 2 * (lane - half) + 1)
    return (blk + 1) * CMP_BLOCK - 1


def _top_blocks(score, k_top):
    lane = _iota(score.shape, 1)
    work = score
    picked = jnp.zeros(score.shape, jnp.bool_)
    for _ in range(k_top):
        hit = lane == jnp.argmax(work, axis=1, keepdims=True).astype(jnp.int32)
        picked = picked | hit
        work = jnp.where(hit, -jnp.inf, work)
    return picked


def _select_mask(imp, tpos):
    nb = imp.shape[1]
    b = _iota((1, nb), 1)
    cur = tpos >> 6
    forced = (b == 0) | (b == cur) | (b == cur - 1)
    valid = (b * SEL_BLOCK) <= tpos
    score = jnp.where(valid, imp + FORCE_BONUS * forced.astype(F32), NEG)
    return _top_blocks(score, min(N_SEL, nb)) & valid


NSA_TQ = 128
NSA_TK = 1024
NSA_R = NSA_HEADS // NSA_KV
LOG2E = math.log2(math.e)


def _nsa_prompt_kernel(q_ref, gt_ref, kc_ref, vc_ref, ka_ref, vst_ref, kw_ref, vw_ref, o_ref,
                       m_ref, l_ref, acc_ref):
    i = pl.program_id(1)
    s0 = i * NSA_TQ
    rows = NSA_R * NSA_TQ
    q = q_ref[...]
    qr = jnp.concatenate([q[:, r * HEAD_DIM:(r + 1) * HEAD_DIM] for r in range(NSA_R)], axis=0).astype(BF16)
    tpos_q = s0 + _iota((NSA_TQ, 1), 0)
    tpos = jnp.concatenate([tpos_q] * NSA_R, axis=0)

    nc = kc_ref.shape[0]
    nb = nc // 2
    sc = _dot_nt(qr, kc_ref[...].astype(BF16)) * (SCALE * LOG2E)
    pc = _softmax_masked(sc, _cmp_block_end(nc) <= tpos)
    o_c = _dot(pc.astype(BF16), vc_ref[...].astype(BF16))
    pcs = pc[0:NSA_TQ]
    for r in range(1, NSA_R):
        pcs = pcs + pc[r * NSA_TQ:(r + 1) * NSA_TQ]
    imp = pcs[:, :nb] + pcs[:, nb:]

    wlen = WINDOW + NSA_TQ
    w0 = pl.multiple_of(jnp.maximum(s0 - WINDOW, 0), NSA_TQ)
    sw = _dot_nt(qr, kw_ref[pl.ds(w0, wlen), :]) * (SCALE * LOG2E)
    kpos = w0 + _iota((1, wlen), 1)
    pw = _softmax_masked(sw, (kpos <= tpos) & (kpos > tpos - WINDOW))
    o_w = _dot(pw.astype(BF16), vw_ref[pl.ds(w0, wlen), :])

    sel = _select_mask(imp, tpos_q)
    bias_t = jnp.where(sel, 0.0, NEG).T
    q_t = jnp.concatenate([q[:, r * HEAD_DIM:(r + 1) * HEAD_DIM].T for r in range(NSA_R)], axis=1)
    q_aug_t = jnp.concatenate([q_t, jnp.concatenate([bias_t] * NSA_R, axis=1)], axis=0).astype(BF16)
    tpos_l = s0 + (_iota((1, rows), 1) & (NSA_TQ - 1))
    m_ref[...] = jnp.full((1, rows), M_FLOOR, F32)
    l_ref[...] = jnp.zeros((1, rows), F32)
    acc_ref[...] = jnp.zeros((HEAD_DIM, rows), F32)

    def tile(kt, masked):
        s = _dot(ka_ref[0, pl.ds(pl.multiple_of(kt * NSA_TK, NSA_TK), NSA_TK), :], q_aug_t)
        if masked:
            s = jnp.where(kt * NSA_TK + _iota((NSA_TK, 1), 0) <= tpos_l, s, NEG)
        m_old = m_ref[...]
        m_new = jnp.maximum(m_old, jnp.max(s, axis=0, keepdims=True))
        p = jnp.exp2((s - m_new) * (SCALE * LOG2E))
        alpha = jnp.exp2((m_old - m_new) * (SCALE * LOG2E))
        l_ref[...] = alpha * l_ref[...] + jnp.sum(p, axis=0, keepdims=True)
        acc_ref[...] = alpha * acc_ref[...] + _dot(vst_ref[0, kt], p.astype(BF16))
        m_ref[...] = m_new

    def pair_body(j, carry):
        tile(2 * j, False)
        tile(2 * j + 1, False)
        return carry

    n_full = s0 >> 10
    lax.fori_loop(0, n_full >> 1, pair_body, 0)

    @pl.when((n_full & 1) == 1)
    def _():
        tile(n_full - 1, False)

    tile(n_full, True)
    o_s_t = acc_ref[...] * (1.0 / l_ref[...])

    gt = gt_ref[0]
    outs = []
    for r in range(NSA_R):
        rs = slice(r * NSA_TQ, (r + 1) * NSA_TQ)
        outs.append(gt[:, 3 * r:3 * r + 1] * o_c[rs] + gt[:, 3 * r + 1:3 * r + 2] * o_s_t[:, rs].T
                    + gt[:, 3 * r + 2:3 * r + 3] * o_w[rs])
    o_ref[...] = jnp.concatenate(outs, axis=1).astype(o_ref.dtype)


def _nsa_prompt(z, gates_t, k_cmp, v_cmp, k_aug, vs_t, kvb, t, kw_col, vw_col):
    assert t % NSA_TK == 0 and t >= WINDOW + NSA_TQ and NSA_TQ == HEAD_DIM
    nc = k_cmp.shape[0]
    rows = NSA_R * NSA_TQ
    return pl.pallas_call(
        _nsa_prompt_kernel, grid=(NSA_KV, t // NSA_TQ),
        in_specs=[pl.BlockSpec((NSA_TQ, NSA_R * HEAD_DIM), lambda g, i: (i, g)),
                  pl.BlockSpec((1, NSA_TQ, 3 * NSA_R), lambda g, i: (g, i, 0)),
                  pl.BlockSpec((nc, HEAD_DIM), lambda g, i: (0, g)),
                  pl.BlockSpec((nc, HEAD_DIM), lambda g, i: (0, g)),
                  pl.BlockSpec((1, t, 2 * HEAD_DIM), lambda g, i: (g, 0, 0)),
                  pl.BlockSpec((1,) + vs_t.shape[1:], lambda g, i: (g, 0, 0, 0)),
                  pl.BlockSpec((t, HEAD_DIM), lambda g, i: (0, kw_col + g)),
                  pl.BlockSpec((t, HEAD_DIM), lambda g, i: (0, vw_col + g))],
        out_specs=pl.BlockSpec((NSA_TQ, NSA_R * HEAD_DIM), lambda g, i: (i, g)),
        out_shape=jax.ShapeDtypeStruct((t, NSA_HEADS * HEAD_DIM), BF16),
        scratch_shapes=[pltpu.VMEM((1, rows), F32), pltpu.VMEM((1, rows), F32),
                        pltpu.VMEM((HEAD_DIM, rows), F32)],
        compiler_params=_cparams(("arbitrary", "arbitrary")),
        name="nsa_prompt",
    )(z, gates_t, k_cmp, v_cmp, k_aug, vs_t, kvb, kvb)


SB_T = 256
SB_R = SB_HEADS // NSA_KV


def _strict_upper(n):
    return (_iota((n, n), 0) > _iota((n, n), 1)).astype(BF16)


def _log2_sigmoids(z2):
    lb = jnp.minimum(z2, 0.0) - jnp.log2(1.0 + jnp.exp2(-jnp.abs(z2)))
    return lb, lb - z2


def _sb_tile_t(z_t, causal_t, carry, low):
    lb, lk = _log2_sigmoids(z_t)
    if causal_t is not None:
        lk = jnp.where(causal_t, lk, 0.0)
    after = _dot(low, lk.astype(BF16)) + carry
    a = jnp.exp2(lb + after)
    if causal_t is not None:
        a = jnp.where(causal_t, a, 0.0)
    return a, carry + jnp.sum(lk, axis=0, keepdims=True)


def _sb_prompt_kernel(qt_ref, k_ref, vt_ref, o_ref, carry_ref, acc_ref):
    i = pl.program_id(1)
    rows = SB_R * SB_T
    q_t = qt_ref[0, 0]
    low = (_iota((SB_T, SB_T), 0) < _iota((SB_T, SB_T), 1)).astype(BF16)

    causal_t = _iota((SB_T, 1), 0) < (_iota((1, rows), 1) & (SB_T - 1))
    z = _dot(k_ref[pl.ds(pl.multiple_of(i * SB_T, SB_T), SB_T), :], q_t)
    a, carry = _sb_tile_t(z, causal_t, jnp.zeros((1, rows), F32), low)
    carry_ref[...] = carry
    acc_ref[...] = _dot(vt_ref[0, i], a.astype(BF16))

    def left_tile(kj):
        zt = _dot(k_ref[pl.ds(pl.multiple_of(kj * SB_T, SB_T), SB_T), :], q_t)
        at, cn = _sb_tile_t(zt, None, carry_ref[...], low)
        carry_ref[...] = cn
        acc_ref[...] += _dot(vt_ref[0, kj], at.astype(BF16))

    def pair_body(j, c):
        left_tile(i - 1 - 2 * j)
        left_tile(i - 2 - 2 * j)
        return c

    lax.fori_loop(0, i >> 1, pair_body, 0)

    @pl.when((i & 1) == 1)
    def _():
        left_tile(0)
    acc = acc_ref[...]
    o_ref[...] = jnp.concatenate([acc[:, r * SB_T:(r + 1) * SB_T].T for r in range(SB_R)],
                                 axis=1).astype(o_ref.dtype)


def _sb_prompt(sq_t, sk, sv_t):
    t = sk.shape[0]
    assert t % SB_T == 0 and (SB_T & (SB_T - 1)) == 0
    rows = SB_R * SB_T
    return pl.pallas_call(
        _sb_prompt_kernel, grid=(NSA_KV, t // SB_T),
        in_specs=[pl.BlockSpec((1, 1, HEAD_DIM, rows), lambda g, i: (g, i, 0, 0)),
                  pl.BlockSpec((t, HEAD_DIM), lambda g, i: (0, g)),
                  pl.BlockSpec((1,) + sv_t.shape[1:], lambda g, i: (g, 0, 0, 0))],
        out_specs=pl.BlockSpec((SB_T, SB_R * HEAD_DIM), lambda g, i: (i, g)),
        out_shape=jax.ShapeDtypeStruct((t, SB_HEADS * HEAD_DIM), BF16),
        scratch_shapes=[pltpu.VMEM((1, rows), F32), pltpu.VMEM((HEAD_DIM, rows), F32)],
        compiler_params=_cparams(("arbitrary", "arbitrary")),
        name="sb_prompt",
    )(sq_t, sk, sv_t)


MEM_TQ = 512


def _mem_prompt_kernel(q_ref, k_ref, v_ref, o_ref):
    outs = []
    for h in range(MEM_HEADS):
        hs = slice(h * HEAD_DIM, (h + 1) * HEAD_DIM)
        s = _dot_nt(q_ref[:, hs].astype(BF16), k_ref[:, hs].astype(BF16)) * SCALE
        e = jnp.exp(s - jnp.max(s, axis=-1, keepdims=True))
        p = e * (1.0 / jnp.sum(e, axis=-1, keepdims=True))
        outs.append(_dot(p.astype(BF16), v_ref[:, hs].astype(BF16)))
    o_ref[...] = jnp.concatenate(outs, axis=1).astype(o_ref.dtype)


def _mem_prompt(z, mq_col, mk, mv, t):
    n_mem = mk.shape[0]
    w = MEM_HEADS * HEAD_DIM
    return pl.pallas_call(
        _mem_prompt_kernel, grid=(t // MEM_TQ,),
        in_specs=[pl.BlockSpec((MEM_TQ, w), lambda i: (i, mq_col)),
                  pl.BlockSpec((n_mem, w), lambda i: (0, 0)),
                  pl.BlockSpec((n_mem, w), lambda i: (0, 0))],
        out_specs=pl.BlockSpec((MEM_TQ, w), lambda i: (i, 0)),
        out_shape=jax.ShapeDtypeStruct((t, w), BF16),
        compiler_params=_cparams(("arbitrary",)),
        name="mem_prompt",
    )(z, mk, mv)


HROWS = 16


def _rows_of_group(n_rows, g, heads_per_group):
    row = _iota((n_rows, 1), 0)
    return (row >= g * heads_per_group) & (row < (g + 1) * heads_per_group)


def _group_lanes(full, heads_per_group):
    out = jnp.zeros((full.shape[0], HEAD_DIM), F32)
    for g in range(NSA_KV):
        out = out + jnp.where(_rows_of_group(full.shape[0], g, heads_per_group),
                              full[:, g * HEAD_DIM:(g + 1) * HEAD_DIM], 0.0)
    return out


def _token_rows(ref, n_tok, lead=()):
    return jnp.concatenate([ref[(*lead, pl.ds(g, n_tok, stride=NSA_KV), slice(None))] for g in range(NSA_KV)],
                           axis=1).astype(BF16)


def _page_specs(n_pages):
    return [pl.BlockSpec((PAGE * NSA_KV, HEAD_DIM), lambda b, pt, c=c: (pt[b * n_pages + c], 0))
            for c in range(n_pages)]


def _dec_sb_kernel(pt_ref, q_ref, *refs, n_pages):
    del pt_ref
    k_refs, v_refs, o_ref = refs[:n_pages], refs[n_pages:2 * n_pages], refs[2 * n_pages]
    q = q_ref[0]
    u = _strict_upper(PAGE)
    z = jnp.concatenate([_dot_nt(q, _token_rows(r, PAGE)) for r in k_refs], axis=1)
    lb, lk = _log2_sigmoids(z)
    lkb = lk.astype(BF16)
    page = lambda x, c: x[:, c * PAGE:(c + 1) * PAGE]
    carry = jnp.zeros((HROWS, 1), F32)
    acc = jnp.zeros((HROWS, KVW), F32)
    for c in reversed(range(n_pages)):
        after = _dot(page(lkb, c), u) + carry
        a = jnp.exp2(page(lb, c) + after)
        acc = acc + _dot(a.astype(BF16), _token_rows(v_refs[c], PAGE))
        carry = carry + jnp.sum(page(lk, c), axis=1, keepdims=True)
    o_ref[0] = _group_lanes(acc, SB_R).astype(o_ref.dtype)


def _dec_sb(page_ids, q_rows, cache_k, cache_v, n_pages):
    nb = q_rows.shape[0]
    grid_spec = pltpu.PrefetchScalarGridSpec(
        num_scalar_prefetch=1, grid=(nb,),
        in_specs=[pl.BlockSpec((1, HROWS, KVW), lambda b, pt: (b, 0, 0))] + 2 * _page_specs(n_pages),
        out_specs=pl.BlockSpec((1, HROWS, HEAD_DIM), lambda b, pt: (b, 0, 0)))
    return pl.pallas_call(
        functools.partial(_dec_sb_kernel, n_pages=n_pages), grid_spec=grid_spec,
        out_shape=jax.ShapeDtypeStruct((nb, HROWS, HEAD_DIM), BF16),
        compiler_params=_cparams(("arbitrary",)),
        name="dec_sb",
    )(page_ids, q_rows, *([cache_k] * n_pages), *([cache_v] * n_pages))


def _bf16_round(x):
    return x.astype(BF16).astype(F32)


def _pad_halves(x, half_rows):
    n = x.shape[0] // 2
    zeros = jnp.zeros((half_rows - n, x.shape[1]), x.dtype)
    return jnp.concatenate([x[:n], zeros, x[n:], zeros], axis=0)


def _dec_nsa_kernel(pt_ref, q_ref, gt_ref, kc_ref, vc_ref, wk_ref, wv_ref, nks_ref, nvs_ref, nkw_ref, nvw_ref,
                    *refs, past, n_pages):
    del pt_ref
    ks_refs, vs_refs, o_ref = refs[:n_pages], refs[n_pages:2 * n_pages], refs[2 * n_pages]
    q = q_ref[0]
    qf = q.astype(F32)

    nb = HEAD_DIM
    kc = _pad_halves(kc_ref[0].astype(BF16), nb)
    vc = _pad_halves(vc_ref[0].astype(BF16), nb)
    sc = _dot_nt(q, kc) * (SCALE * LOG2E)
    pc = _softmax_masked(sc, _cmp_block_end(2 * nb) <= past)
    o_c = _group_lanes(_dot(pc.astype(BF16), vc), NSA_R)
    imp = jnp.zeros((8, nb), F32)
    for g in range(NSA_KV):
        pcs = jnp.sum(pc[g * NSA_R:(g + 1) * NSA_R], axis=0, keepdims=True)
        imp = imp + jnp.where(_iota((8, 1), 0) == g, pcs[:, :nb] + pcs[:, nb:], 0.0)
    sel = _select_mask(imp, jnp.full((1, 1), past, jnp.int32)).astype(F32)
    pk = jnp.zeros((HROWS, nb), F32)
    for g in range(NSA_KV):
        pk = pk + jnp.where(_rows_of_group(HROWS, g, NSA_R), sel[g:g + 1, :], 0.0)

    s = jnp.concatenate([_dot_nt(q, _token_rows(r, PAGE)) for r in ks_refs], axis=1) * SCALE
    expand = (_iota((nb, past), 0) == (_iota((nb, past), 1) >> 6)).astype(BF16)
    s = jnp.where(_dot(pk.astype(BF16), expand) > 0.5, s, NEG)
    s_new = jnp.sum(qf * _bf16_round(nks_ref[0]), axis=1, keepdims=True) * SCALE
    nblk = past // SEL_BLOCK
    s_new = jnp.where(pk[:, nblk:nblk + 1] > 0.5, s_new, NEG)
    ms = jnp.maximum(jnp.maximum(jnp.max(s, axis=1, keepdims=True), s_new), M_FLOOR)
    e = jnp.exp(s - ms)
    e_new = jnp.exp(s_new - ms)
    ls = jnp.sum(e, axis=1, keepdims=True) + e_new
    acc = _bf16_round(e_new) * _bf16_round(nvs_ref[0])
    for c in range(n_pages):
        acc = acc + _dot(e[:, c * PAGE:(c + 1) * PAGE].astype(BF16), _token_rows(vs_refs[c], PAGE))
    o_s = _group_lanes(acc, NSA_R) * (1.0 / ls)

    wl = wk_ref.shape[1] // NSA_KV
    sw = _dot_nt(q, _token_rows(wk_ref, wl, (0,))) * SCALE
    wmask = past - wl + _iota((1, wl), 1) > past - WINDOW
    sw = jnp.where(wmask, sw, NEG)
    sw_new = jnp.sum(qf * _bf16_round(nkw_ref[0]), axis=1, keepdims=True) * SCALE
    mw = jnp.maximum(jnp.max(sw, axis=1, keepdims=True), sw_new)
    ew = jnp.where(wmask, jnp.exp(sw - mw), 0.0)
    ew_new = jnp.exp(sw_new - mw)
    lw = jnp.sum(ew, axis=1, keepdims=True) + ew_new
    accw = _dot(ew.astype(BF16), _token_rows(wv_ref, wl, (0,))) + _bf16_round(ew_new) * _bf16_round(nvw_ref[0])
    o_w = _group_lanes(accw, NSA_R) * (1.0 / lw)

    gt = gt_ref[0]
    o_ref[0] = (gt[:, 0:1] * o_c + gt[:, 1:2] * o_s + gt[:, 2:3] * o_w).astype(o_ref.dtype)


def _dec_nsa(page_ids, q_rows, gates, k_cmp, v_cmp, cache_k, cache_v, win_k, win_v, new_ks, new_vs, new_kw,
             new_vw, n_pages):
    nb = q_rows.shape[0]
    nc = k_cmp.shape[1]
    past = n_pages * PAGE
    assert nc <= 2 * HEAD_DIM and past // SEL_BLOCK < HEAD_DIM
    per_b = lambda shape: pl.BlockSpec((1,) + shape, lambda b, pt: (b, 0, 0))
    grid_spec = pltpu.PrefetchScalarGridSpec(
        num_scalar_prefetch=1, grid=(nb,),
        in_specs=[per_b((HROWS, KVW)), per_b((HROWS, 3)), per_b((nc, KVW)), per_b((nc, KVW)),
                  per_b(win_k.shape[1:]), per_b(win_v.shape[1:]),
                  per_b((1, KVW)), per_b((1, KVW)), per_b((1, KVW)), per_b((1, KVW))] + 2 * _page_specs(n_pages),
        out_specs=per_b((HROWS, HEAD_DIM)))
    return pl.pallas_call(
        functools.partial(_dec_nsa_kernel, past=past, n_pages=n_pages), grid_spec=grid_spec,
        out_shape=jax.ShapeDtypeStruct((nb, HROWS, HEAD_DIM), BF16),
        compiler_params=_cparams(("arbitrary",)),
        name="dec_nsa",
    )(page_ids, q_rows, gates, k_cmp, v_cmp, win_k, win_v, new_ks, new_vs, new_kw, new_vw,
      *([cache_k] * n_pages), *([cache_v] * n_pages))


def _dec_mem_kernel(q_ref, k_ref, v_ref, o_ref):
    n_mem = k_ref.shape[1] // NSA_KV
    s = _dot_nt(q_ref[0], _token_rows(k_ref, n_mem, (0,))) * SCALE
    e = jnp.exp(s - jnp.max(s, axis=-1, keepdims=True))
    p = e * (1.0 / jnp.sum(e, axis=-1, keepdims=True))
    o_ref[0] = _group_lanes(_dot(p.astype(BF16), _token_rows(v_ref, n_mem, (0,))), 1).astype(o_ref.dtype)


def _dec_mem(q_rows, mem_k, mem_v):
    nb = mem_k.shape[0]
    rows = q_rows.shape[1]
    per_b = lambda shape: pl.BlockSpec((1,) + shape, lambda b: (b, 0, 0))
    return pl.pallas_call(
        _dec_mem_kernel, grid=(nb,),
        in_specs=[per_b((rows, KVW)), per_b(mem_k.shape[1:]), per_b(mem_v.shape[1:])],
        out_specs=per_b((rows, HEAD_DIM)),
        out_shape=jax.ShapeDtypeStruct((nb, rows, HEAD_DIM), BF16),
        compiler_params=_cparams(("arbitrary",)),
        name="dec_mem",
    )(q_rows, mem_k, mem_v)


def _head_rows(q, n_heads, heads_per_group, rows):
    nb = q.shape[0]
    qh = q.reshape(nb, n_heads, 1, HEAD_DIM)
    onehot = (jnp.arange(n_heads)[:, None] // heads_per_group == jnp.arange(NSA_KV)[None, :])
    full = jnp.where(onehot[None, :, :, None], qh, 0.0).reshape(nb, n_heads, KVW)
    return jnp.pad(full, ((0, 0), (0, rows - n_heads), (0, 0))).astype(BF16)


def _even_odd(x, lanes_half):
    n = x.shape[-2]
    pad = [(0, 0)] * (x.ndim - 2) + [(0, lanes_half - n // 2), (0, 0)]
    return jnp.concatenate([jnp.pad(x[..., 0::2, :], pad), jnp.pad(x[..., 1::2, :], pad)], axis=-2)


def _rope_tables(pos):
    half = ROPE_DIMS // 2
    inv = jnp.power(ROPE_THETA, -jnp.arange(half, dtype=F32) * 2.0 / ROPE_DIMS)
    ang = pos.astype(F32)[:, None] * inv[None, :]
    cos, sin = jnp.cos(ang), jnp.sin(ang)
    n = pos.shape[0]
    rest = HEAD_DIM - ROPE_DIMS
    c = jnp.concatenate([cos, cos, jnp.ones((n, rest), F32)], axis=1)
    s_lo = jnp.concatenate([-sin, jnp.zeros((n, HEAD_DIM - half), F32)], axis=1)
    s_hi = jnp.concatenate([jnp.zeros((n, half), F32), sin, jnp.zeros((n, rest), F32)], axis=1)
    return c, s_lo, s_hi


C_Q, C_KC, C_VC, C_KS, C_VS, C_KW, C_VW = 0, 2048, 2560, 3072, 3584, 4096, 4608
C_SQ, C_SK, C_SV, C_MQ, C_BG, C_END = 5120, 6656, 7168, 7680, 8192, 20480
NG_COLS = NSA_HEADS * 3
NG_SRC = 5120


def kernel(x_prompt, mem_prompt, x_sample, cache_cmp_k, cache_cmp_v, cache_sel_k, cache_sel_v, cache_sb_k,
           cache_sb_v, state_win_k, state_win_v, cache_mem_k, cache_mem_v, page_table, norm1_w, w_in, g_nsa_q,
           g_cmp_k, g_sel_k, g_win_k, g_mem_q, g_mem_k, mem_norm_w, w_mem_kv, cmp_pos_k, cmp_k_w1, cmp_k_w2,
           cmp_pos_v, cmp_v_w1, cmp_v_w2, w_up_nsa, w_up_sb, w_up_mem, w_o, norm2_w, w_ff1, w_ff2):
    tp = x_prompt.shape[1]
    nb = x_sample.shape[0]
    n_pages = page_table.shape[1]
    past = n_pages * PAGE
    m = tp + nb
    tm = _row_tile(m, 832)
    assert x_prompt.shape[0] == 1 and x_sample.shape[1] == 1

    h = _rms_stack(x_prompt[0], x_sample[:, 0], norm1_w)
    w_a = w_in[:, :NG_SRC].astype(BF16)
    w_b = w_in[:, NG_SRC + NG_COLS:].astype(BF16)
    w_ng = jnp.pad(w_in[:, NG_SRC:NG_SRC + NG_COLS], ((0, 0), (0, HEAD_DIM - NG_COLS))).astype(BF16)
    pos = jnp.concatenate([jnp.arange(tp, dtype=jnp.int32), jnp.full((nb,), past, jnp.int32)])
    rope_c, rope_lo, rope_hi = _rope_tables(pos)
    modes = np.zeros((C_END // IN_TN,), np.int32)
    gidx = np.zeros((C_END // IN_TN,), np.int32)
    for c0, c1, md, gi in ((C_Q, C_KC, MODE_NORM_ROPE, 0), (C_KC, C_VC, MODE_NORM_ROPE, 1),
                           (C_KS, C_VS, MODE_NORM_ROPE, 2), (C_KW, C_VW, MODE_NORM_ROPE, 3),
                           (C_MQ, C_BG, MODE_NORM, 4)):
        modes[c0 // IN_TN:c1 // IN_TN] = md
        gidx[c0 // IN_TN:c1 // IN_TN] = gi
    gains = jnp.stack([g_nsa_q, g_cmp_k, g_sel_k, g_win_k, g_mem_q]).reshape(5, 1, HEAD_DIM)
    na = C_SQ // IN_TN
    z_a = _inproj(h, w_a, jnp.asarray(modes[:na]), jnp.asarray(gidx[:na]), gains, rope_c, rope_lo, rope_hi, tm)
    z_b = _inproj(h, w_b, jnp.asarray(modes[na:]), jnp.asarray(gidx[na:]), gains, rope_c, rope_lo, rope_hi, tm)
    ng = _mm(h, w_ng, tm=tm, tn=HEAD_DIM, tk=D_MODEL, out_dtype=F32, epi=_epi_sigmoid, name="head_gates")

    def zcols(c0, c1, r0, r1):
        arr, off = (z_a, 0) if c0 < C_SQ else (z_b, C_SQ)
        return arr[r0:r1, c0 - off:c1 - off]

    zp = lambda c0, c1: zcols(c0, c1, 0, tp)
    zs = lambda c0, c1: zcols(c0, c1, tp, m)
    kv_out = lambda a: a.reshape(a.shape[0], NSA_KV, HEAD_DIM)
    kv_cols = (C_KC, C_VC, C_KS, C_VS, C_SK, C_SV)
    p_kv = [kv_out(zp(c, c + KVW))[None] for c in kv_cols]
    s_kv = [kv_out(zs(c, c + KVW))[:, None] for c in kv_cols]
    keep = min(WINDOW, tp)
    p_win = [kv_out(zcols(c, c + KVW, tp - keep, tp))[None] for c in (C_KW, C_VW)]
    new_win = [kv_out(zs(c, c + KVW))[:, None] for c in (C_KW, C_VW)]
    lb = state_win_k.shape[1]
    keep_s = min(WINDOW, past + 1)
    s_win = [jnp.concatenate([st, nw], axis=1)[:, lb + 1 - keep_s:]
             for st, nw in ((state_win_k, new_win[0]), (state_win_v, new_win[1]))]

    w1k, w2k = cmp_k_w1.astype(BF16), cmp_k_w2.astype(BF16)
    w1v, w2v = cmp_v_w1.astype(BF16), cmp_v_w2.astype(BF16)
    ids_p = jnp.arange(tp // PAGE, dtype=jnp.int32)
    n_cmp = tp // CMP_BLOCK
    token_rows = lambda a: a.reshape(-1, HEAD_DIM)
    k_cmp_p = _even_odd(_compress(token_rows(p_kv[0]), ids_p, cmp_pos_k, w1k, w2k).reshape(n_cmp, KVW), HEAD_DIM)
    v_cmp_p = _even_odd(_compress(token_rows(p_kv[1]), ids_p, cmp_pos_v, w1v, w2v).reshape(n_cmp, KVW), HEAD_DIM)

    gates_all = ng[:, :NG_COLS].reshape(m, NSA_KV, 3 * NSA_R)
    gates_t = jnp.transpose(gates_all[:tp], (1, 0, 2))
    kvb = zp(C_KC, C_SQ).astype(BF16)
    blk_of_key = jnp.arange(tp, dtype=jnp.int32) // SEL_BLOCK
    onehot = (blk_of_key[:, None] == jnp.arange(HEAD_DIM, dtype=jnp.int32)[None, :]).astype(BF16)
    ks_b = jnp.transpose(zp(C_KS, C_VS).astype(BF16).reshape(tp, NSA_KV, HEAD_DIM), (1, 0, 2))
    k_aug = jnp.concatenate([ks_b, jnp.broadcast_to(onehot[None], ks_b.shape)], axis=2)
    vs_t = jnp.transpose(zp(C_VS, C_KW).astype(BF16).reshape(tp // NSA_TK, NSA_TK, NSA_KV, HEAD_DIM), (2, 0, 3, 1))
    cb = lambda c0: (c0 - C_KC) // HEAD_DIM
    o_nsa_p = _nsa_prompt(z_a, gates_t, k_cmp_p, v_cmp_p, k_aug, vs_t, kvb, tp, cb(C_KW), cb(C_VW))

    nq = tp // SB_T
    sq_t = jnp.transpose((zp(C_SQ, C_SK) * (SCALE * LOG2E)).astype(BF16).reshape(nq, SB_T, NSA_KV, SB_R, HEAD_DIM),
                         (2, 0, 4, 3, 1)).reshape(NSA_KV, nq, HEAD_DIM, SB_R * SB_T)
    sv_t = jnp.transpose(zp(C_SV, C_MQ).astype(BF16).reshape(nq, SB_T, NSA_KV, HEAD_DIM), (2, 0, 3, 1))
    o_sb_p = _sb_prompt(sq_t, zp(C_SK, C_SV).astype(BF16), sv_t)

    mem_h = _rms(mem_prompt[0], mem_norm_w, mem_prompt.shape[1])
    w_mkv = w_mem_kv.astype(BF16)
    n_mem = mem_prompt.shape[1]
    mw = MEM_HEADS * HEAD_DIM
    mk = _mm(mem_h, w_mkv, tm=n_mem, tn=mw, tk=D_MODEL, out_dtype=F32, epi=_epi_headnorm,
             extras=((g_mem_k.reshape(1, HEAD_DIM), 0),), ncols=mw, name="mem_k")
    mv = _mm(mem_h, w_mkv, tm=n_mem, tn=mw, tk=D_MODEL, out_dtype=F32, col0=1, name="mem_v")
    o_mem_p = _mem_prompt(z_b, (C_MQ - C_SQ) // mw, mk, mv, tp)

    pt = page_table.reshape(-1).astype(jnp.int32)
    n_cmp_s = past // CMP_BLOCK
    k_cmp_s = _even_odd(_compress(token_rows(cache_cmp_k), pt, cmp_pos_k, w1k, w2k), n_cmp_s // 2)
    v_cmp_s = _even_odd(_compress(token_rows(cache_cmp_v), pt, cmp_pos_v, w1v, w2v), n_cmp_s // 2)
    q_nsa_s = _head_rows(zs(C_Q, C_KC), NSA_HEADS, NSA_R, HROWS)
    gates_s = ng[tp:, :NG_COLS].reshape(nb, NSA_HEADS, 3)
    seq_rows = lambda a: a.reshape(a.shape[0], -1, HEAD_DIM)
    new_row = lambda c0: zs(c0, c0 + KVW).reshape(nb, 1, KVW)
    o_nsa_s = _dec_nsa(pt, q_nsa_s, gates_s, k_cmp_s, v_cmp_s, token_rows(cache_sel_k), token_rows(cache_sel_v),
                       seq_rows(state_win_k), seq_rows(state_win_v), new_row(C_KS), new_row(C_VS), new_row(C_KW),
                       new_row(C_VW), n_pages)
    q_sb_s = _head_rows(zs(C_SQ, C_SK) * (SCALE * LOG2E), SB_HEADS, SB_R, HROWS)
    o_sb_s = _dec_sb(pt, q_sb_s, token_rows(cache_sb_k), token_rows(cache_sb_v), n_pages)
    q_mem_s = _head_rows(zs(C_MQ, C_BG), MEM_HEADS, 1, 8)
    o_mem_s = _dec_mem(q_mem_s, seq_rows(cache_mem_k), seq_rows(cache_mem_v))

    o_nsa = jnp.concatenate([o_nsa_p, o_nsa_s.reshape(nb, NSA_HEADS * HEAD_DIM)], axis=0)
    o_sb = jnp.concatenate([o_sb_p, o_sb_s[:, :SB_HEADS].reshape(nb, SB_HEADS * HEAD_DIM)], axis=0)
    o_mem = jnp.concatenate([o_mem_p, o_mem_s[:, :MEM_HEADS].reshape(nb, mw)], axis=0)

    u = _merge_up(o_nsa, o_sb, o_mem, w_up_nsa.astype(BF16), w_up_sb.astype(BF16), w_up_mem.astype(BF16),
                  z_b, C_BG - C_SQ, tm, 512)
    x_all = jnp.concatenate([x_prompt[0], x_sample[:, 0]], axis=0)
    x1 = _mm(u, w_o.astype(BF16), tm=tm, tn=1024, tk=D_MODEL, out_dtype=F32, epi=_epi_residual,
             extras=((x_all, 0),), name="out_proj")
    h2 = _rms(x1, norm2_w, _row_tile(m, 512))
    a = _mm(h2, w_ff1.astype(BF16), tm=tm, tn=1024, tk=D_MODEL, out_dtype=BF16, epi=_epi_relu2, name="ff1")
    y = _mm(a, w_ff2.astype(BF16), tm=tm, tn=1024, tk=4096, out_dtype=F32, epi=_epi_residual,
            extras=((x1, 0),), name="ff2")

    y_prompt = y[:tp][None]
    y_sample = y[tp:][:, None]
    p_mem = [mk.reshape(1, n_mem, MEM_HEADS, HEAD_DIM), mv.reshape(1, n_mem, MEM_HEADS, HEAD_DIM)]
    return (y_prompt, y_sample, *p_kv, *p_win, *p_mem, *s_kv, *s_win)
```

```python
import functools
import math

import numpy as np
import jax
import jax.numpy as jnp
from jax import lax
from jax.experimental import pallas as pl
from jax.experimental.pallas import tpu as pltpu

F32 = jnp.float32
BF16 = jnp.bfloat16

D_MODEL = 4096
HEAD_DIM = 128
NSA_HEADS = 16
NSA_KV = 4
SB_HEADS = 12
MEM_HEADS = 4
CMP_BLOCK = 32
SEL_BLOCK = 64
N_SEL = 16
WINDOW = 512
PAGE = 128
ROPE_THETA = 500000.0
ROPE_DIMS = HEAD_DIM // 4
EPS = 1e-6
NEG = -1e30
M_FLOOR = -1e20
FORCE_BONUS = 1e4
SCALE = HEAD_DIM ** -0.5
KVW = NSA_KV * HEAD_DIM

VMEM_LIMIT = 56 * 1024 * 1024


def _row_tile(m, cap):
    for step in (16, 8):
        best = max((t for t in range(step, cap + 1, step) if m % t == 0), default=0)
        if best:
            return best
    raise ValueError(f"no row tile for {m}")


def _cparams(sem, vmem=VMEM_LIMIT):
    return pltpu.CompilerParams(dimension_semantics=sem, vmem_limit_bytes=vmem)


def _dot(a, b):
    return jnp.dot(a, b, preferred_element_type=F32)


def _dot_nt(a, b):
    return lax.dot_general(a, b, (((1,), (1,)), ((), ())), preferred_element_type=F32)


def _iota(shape, dim):
    return lax.broadcasted_iota(jnp.int32, shape, dim)


def _rms2_kernel(xp_ref, xs_ref, w_ref, o_ref, *, n_prompt_tiles):
    i = pl.program_id(0)
    x = jnp.where(i < n_prompt_tiles, xp_ref[...], xs_ref[...])
    ms = jnp.mean(x * x, axis=-1, keepdims=True)
    o_ref[...] = (x * lax.rsqrt(ms + EPS) * w_ref[...]).astype(o_ref.dtype)


def _rms_stack(xp, xs, w):
    tp, d = xp.shape
    ts = xs.shape[0]
    assert tp % ts == 0 and ts % 8 == 0
    npt = tp // ts
    return pl.pallas_call(
        functools.partial(_rms2_kernel, n_prompt_tiles=npt),
        grid=(npt + 1,),
        in_specs=[pl.BlockSpec((ts, d), lambda i: (jnp.minimum(i, npt - 1), 0)),
                  pl.BlockSpec((ts, d), lambda i: (0, 0)),
                  pl.BlockSpec((1, d), lambda i: (0, 0))],
        out_specs=pl.BlockSpec((ts, d), lambda i: (i, 0)),
        out_shape=jax.ShapeDtypeStruct((tp + ts, d), BF16),
        compiler_params=_cparams(("arbitrary",)),
        name="rms_stack",
    )(xp, xs, w.reshape(1, d))


def _rms_kernel(x_ref, w_ref, o_ref):
    x = x_ref[...]
    ms = jnp.mean(x * x, axis=-1, keepdims=True)
    o_ref[...] = (x * lax.rsqrt(ms + EPS) * w_ref[...]).astype(o_ref.dtype)


def _rms(x, w, tm):
    m, d = x.shape
    return pl.pallas_call(
        _rms_kernel, grid=(m // tm,),
        in_specs=[pl.BlockSpec((tm, d), lambda i: (i, 0)), pl.BlockSpec((1, d), lambda i: (0, 0))],
        out_specs=pl.BlockSpec((tm, d), lambda i: (i, 0)),
        out_shape=jax.ShapeDtypeStruct((m, d), BF16),
        compiler_params=_cparams(("arbitrary",)),
        name="rms",
    )(x, w.reshape(1, d))


MODE_PLAIN, MODE_NORM_ROPE, MODE_NORM = 0, 1, 2
IN_TN = 512
IN_PARTS = 2


def _head_norm(z, g):
    outs = []
    for hh in range(z.shape[1] // HEAD_DIM):
        xs = z[:, hh * HEAD_DIM:(hh + 1) * HEAD_DIM]
        ms = jnp.mean(xs * xs, axis=-1, keepdims=True)
        outs.append(xs * lax.rsqrt(ms + EPS) * g)
    return outs


def _rope(x, c, s_lo, s_hi):
    half = ROPE_DIMS // 2
    return x * c + pltpu.roll(x, HEAD_DIM - half, 1) * s_lo + pltpu.roll(x, half, 1) * s_hi


def _inproj_kernel(mode_ref, gidx_ref, a_ref, b_ref, g_ref, c_ref, slo_ref, shi_ref, o_ref):
    j = pl.program_id(1)
    o_ref[...] = _dot_nt(a_ref[...], b_ref[...])
    for part in range(IN_PARTS):
        cols = slice(part * IN_TN, (part + 1) * IN_TN)
        mode = mode_ref[j * IN_PARTS + part]
        gain = g_ref[gidx_ref[j * IN_PARTS + part]]

        @pl.when(mode == MODE_NORM)
        def _():
            o_ref[:, cols] = jnp.concatenate(_head_norm(o_ref[:, cols], gain), axis=1)

        @pl.when(mode == MODE_NORM_ROPE)
        def _():
            c, slo, shi = c_ref[...], slo_ref[...], shi_ref[...]
            o_ref[:, cols] = jnp.concatenate([_rope(y, c, slo, shi) for y in _head_norm(o_ref[:, cols], gain)],
                                             axis=1)


def _inproj(h, w_t, modes, gidx, gains, rope_c, rope_lo, rope_hi, tm):
    m, k = h.shape
    n = w_t.shape[0]
    tn = IN_PARTS * IN_TN
    grid_spec = pltpu.PrefetchScalarGridSpec(
        num_scalar_prefetch=2, grid=(m // tm, n // tn),
        in_specs=[pl.BlockSpec((tm, k), lambda i, j, *_: (i, 0)),
                  pl.BlockSpec((tn, k), lambda i, j, *_: (j, 0)),
                  pl.BlockSpec(gains.shape, lambda i, j, *_: (0, 0, 0)),
                  pl.BlockSpec((tm, HEAD_DIM), lambda i, j, *_: (i, 0)),
                  pl.BlockSpec((tm, HEAD_DIM), lambda i, j, *_: (i, 0)),
                  pl.BlockSpec((tm, HEAD_DIM), lambda i, j, *_: (i, 0))],
        out_specs=pl.BlockSpec((tm, tn), lambda i, j, *_: (i, j)))
    return pl.pallas_call(
        _inproj_kernel, grid_spec=grid_spec,
        out_shape=jax.ShapeDtypeStruct((m, n), F32),
        compiler_params=_cparams(("arbitrary", "arbitrary")),
        name="inproj",
    )(modes, gidx, h, w_t, gains, rope_c, rope_lo, rope_hi)


def _mm_kernel(*refs, nk, n_extra, epi):
    a_ref, b_ref = refs[0], refs[1]
    extra = refs[2:2 + n_extra]
    o_ref = refs[2 + n_extra]
    part = _dot(a_ref[...], b_ref[...])
    if nk == 1:
        o_ref[...] = epi(part, *[e[...] for e in extra]).astype(o_ref.dtype)
    else:
        acc_ref = refs[3 + n_extra]
        kk = pl.program_id(2)

        @pl.when(kk == 0)
        def _():
            acc_ref[...] = part

        @pl.when(kk > 0)
        def _():
            acc_ref[...] += part

        @pl.when(kk == nk - 1)
        def _():
            o_ref[...] = epi(acc_ref[...], *[e[...] for e in extra]).astype(o_ref.dtype)


def _mm(a, b, *, tm, tn, tk, out_dtype, epi=None, extras=(), col0=0, ncols=None, name="mm"):
    m, k = a.shape
    ncols = b.shape[1] - col0 * tn if ncols is None else ncols
    assert m % tm == 0 and ncols % tn == 0 and k % tk == 0
    nk = k // tk
    if epi is None:
        epi = lambda z: z
    in_specs = [pl.BlockSpec((tm, tk), lambda i, j, kk: (i, kk)),
                pl.BlockSpec((tk, tn), lambda i, j, kk: (kk, j + col0))]
    args = [a, b]
    for arr, off in extras:
        if arr.shape[0] == 1:
            in_specs.append(pl.BlockSpec((1, arr.shape[1]), lambda i, j, kk: (0, 0)))
        else:
            in_specs.append(pl.BlockSpec((tm, tn), lambda i, j, kk, off=off: (i, j + off)))
        args.append(arr)
    scratch = [pltpu.VMEM((tm, tn), F32)] if nk > 1 else []
    return pl.pallas_call(
        functools.partial(_mm_kernel, nk=nk, n_extra=len(extras), epi=epi),
        grid=(m // tm, ncols // tn, nk),
        in_specs=in_specs,
        out_specs=pl.BlockSpec((tm, tn), lambda i, j, kk: (i, j)),
        out_shape=jax.ShapeDtypeStruct((m, ncols), out_dtype),
        scratch_shapes=scratch,
        compiler_params=_cparams(("arbitrary", "arbitrary", "arbitrary")),
        name=name,
    )(*args)


def _epi_headnorm(z, g):
    return jnp.concatenate(_head_norm(z, g), axis=1)


def _epi_sigmoid(z):
    return jax.nn.sigmoid(z)


def _epi_residual(z, x):
    return x + z


def _epi_relu2(z):
    return jnp.square(jnp.maximum(z, 0.0))


def _up_kernel(on_ref, os_ref, om_ref, wn_ref, ws_ref, wm_ref, g0_ref, g1_ref, g2_ref, o_ref):
    sig = jax.nn.sigmoid
    u = (sig(g0_ref[...]) * _dot(on_ref[...], wn_ref[...])
         + sig(g1_ref[...]) * _dot(os_ref[...], ws_ref[...])
         + sig(g2_ref[...]) * _dot(om_ref[...], wm_ref[...]))
    o_ref[...] = u.astype(o_ref.dtype)


def _merge_up(o_nsa, o_sb, o_mem, w_n, w_s, w_m, z, gate_col0, tm, tn):
    m = o_nsa.shape[0]
    d = w_n.shape[1]
    gb = gate_col0 // tn
    nd = d // tn
    row = lambda kdim: pl.BlockSpec((tm, kdim), lambda i, j: (i, 0))
    col = lambda kdim: pl.BlockSpec((kdim, tn), lambda i, j: (0, j))
    gate = lambda c: pl.BlockSpec((tm, tn), lambda i, j, c=c: (i, gb + c * nd + j))
    return pl.pallas_call(
        _up_kernel, grid=(m // tm, nd),
        in_specs=[row(o_nsa.shape[1]), row(o_sb.shape[1]), row(o_mem.shape[1]),
                  col(w_n.shape[0]), col(w_s.shape[0]), col(w_m.shape[0]),
                  gate(0), gate(1), gate(2)],
        out_specs=pl.BlockSpec((tm, tn), lambda i, j: (i, j)),
        out_shape=jax.ShapeDtypeStruct((m, d), BF16),
        compiler_params=_cparams(("arbitrary", "arbitrary")),
        name="merge_up",
    )(o_nsa, o_sb, o_mem, w_n, w_s, w_m, z, z, z)


CMP_PAGES = 16
CMP_HID = 256
CMP_PITCH = 136


def _gelu_tanh(x):
    cdf = 0.5 * (1.0 + jnp.tanh(math.sqrt(2.0 / math.pi) * (x + 0.044715 * (x * x * x))))
    return x * cdf


def _compress_kernel(pt_ref, *refs):
    del pt_ref
    page_refs = refs[:CMP_PAGES]
    pos_ref, w1_ref, w2_ref, o_ref, buf_ref = refs[CMP_PAGES:]
    brow = CMP_BLOCK * NSA_KV
    for c in range(CMP_PAGES):
        for n in range(PAGE // CMP_BLOCK):
            blk = c * (PAGE // CMP_BLOCK) + n
            buf_ref[blk * CMP_PITCH:blk * CMP_PITCH + brow, :] = page_refs[c][n * brow:(n + 1) * brow, :]
    nrow = CMP_PAGES * (PAGE // CMP_BLOCK)
    acc = jnp.zeros((NSA_KV * nrow, CMP_HID), F32)
    def token(t):
        parts = [buf_ref[pl.ds(t * NSA_KV + g, nrow, stride=CMP_PITCH), :] for g in range(NSA_KV)]
        return (jnp.concatenate(parts, axis=0) + pos_ref[t:t + 1, :]).astype(BF16)

    accs = [acc, acc]
    for i, t in enumerate(range(0, CMP_BLOCK, 2)):
        x = jnp.concatenate([token(t), token(t + 1)], axis=1)
        accs[i % 2] = accs[i % 2] + _dot(x, w1_ref[t * HEAD_DIM:(t + 2) * HEAD_DIM, :])
    acc = accs[0] + accs[1]
    out = _dot(_gelu_tanh(acc).astype(BF16), w2_ref[...])
    for g in range(NSA_KV):
        o_ref[0, :, g * HEAD_DIM:(g + 1) * HEAD_DIM] = out[g * nrow:(g + 1) * nrow, :]


def _compress(pages, page_ids, pos, w1, w2):
    n_steps = page_ids.shape[0] // CMP_PAGES
    nrow = CMP_PAGES * (PAGE // CMP_BLOCK)
    page_specs = [pl.BlockSpec((PAGE * NSA_KV, HEAD_DIM), lambda s, pt, c=c: (pt[s * CMP_PAGES + c], 0))
                  for c in range(CMP_PAGES)]
    grid_spec = pltpu.PrefetchScalarGridSpec(
        num_scalar_prefetch=1, grid=(n_steps,),
        in_specs=page_specs + [pl.BlockSpec(pos.shape, lambda s, pt: (0, 0)),
                               pl.BlockSpec(w1.shape, lambda s, pt: (0, 0)),
                               pl.BlockSpec(w2.shape, lambda s, pt: (0, 0))],
        out_specs=pl.BlockSpec((1, nrow, KVW), lambda s, pt: (s, 0, 0)),
        scratch_shapes=[pltpu.VMEM((nrow * CMP_PITCH, HEAD_DIM), F32)])
    return pl.pallas_call(
        _compress_kernel, grid_spec=grid_spec,
        out_shape=jax.ShapeDtypeStruct((n_steps, nrow, KVW), F32),
        compiler_params=_cparams(("arbitrary",)),
        name="compress",
    )(page_ids, *([pages] * CMP_PAGES), pos, w1, w2)


def _softmax_masked(s2, mask):
    sm = jnp.where(mask, s2, NEG)
    e = jnp.exp2(sm - jnp.max(sm, axis=-1, keepdims=True))
    p = e * (1.0 / jnp.sum(e, axis=-1, keepdims=True))
    return jnp.where(mask, p, 0.0)


def _cmp_block_end(n_lanes):
    half = n_lanes // 2
    lane = _iota((1, n_lanes), 1)
    blk = jnp.where(lane < half, 2 * lane, 2 * (lane - half) + 1)
    return (blk + 1) * CMP_BLOCK - 1


def _top_blocks(score, k_top):
    lane = _iota(score.shape, 1)
    work = score
    picked = jnp.zeros(score.shape, jnp.bool_)
    for _ in range(k_top):
        hit = lane == jnp.argmax(work, axis=1, keepdims=True).astype(jnp.int32)
        picked = picked | hit
        work = jnp.where(hit, -jnp.inf, work)
    return picked


def _select_mask(imp, tpos):
    nb = imp.shape[1]
    b = _iota((1, nb), 1)
    cur = tpos >> 6
    forced = (b == 0) | (b == cur) | (b == cur - 1)
    valid = (b * SEL_BLOCK) <= tpos
    score = jnp.where(valid, imp + FORCE_BONUS * forced.astype(F32), NEG)
    return _top_blocks(score, min(N_SEL, nb)) & valid


NSA_TQ = 128
NSA_TK = 1024
NSA_R = NSA_HEADS // NSA_KV
LOG2E = math.log2(math.e)


def _nsa_prompt_kernel(q_ref, gt_ref, kc_ref, vc_ref, ka_ref, vst_ref, kw_ref, vw_ref, o_ref,
                       m_ref, l_ref, acc_ref):
    i = pl.program_id(1)
    s0 = i * NSA_TQ
    rows = NSA_R * NSA_TQ
    q = q_ref[...]
    qr = jnp.concatenate([q[:, r * HEAD_DIM:(r + 1) * HEAD_DIM] for r in range(NSA_R)], axis=0).astype(BF16)
    tpos_q = s0 + _iota((NSA_TQ, 1), 0)
    tpos = jnp.concatenate([tpos_q] * NSA_R, axis=0)

    nc = kc_ref.shape[0]
    nb = nc // 2
    sc = _dot_nt(qr, kc_ref[...].astype(BF16)) * (SCALE * LOG2E)
    pc = _softmax_masked(sc, _cmp_block_end(nc) <= tpos)
    o_c = _dot(pc.astype(BF16), vc_ref[...].astype(BF16))
    pcs = pc[0:NSA_TQ]
    for r in range(1, NSA_R):
        pcs = pcs + pc[r * NSA_TQ:(r + 1) * NSA_TQ]
    imp = pcs[:, :nb] + pcs[:, nb:]

    wlen = WINDOW + NSA_TQ
    w0 = pl.multiple_of(jnp.maximum(s0 - WINDOW, 0), NSA_TQ)
    sw = _dot_nt(qr, kw_ref[pl.ds(w0, wlen), :]) * (SCALE * LOG2E)
    kpos = w0 + _iota((1, wlen), 1)
    pw = _softmax_masked(sw, (kpos <= tpos) & (kpos > tpos - WINDOW))
    o_w = _dot(pw.astype(BF16), vw_ref[pl.ds(w0, wlen), :])

    sel = _select_mask(imp, tpos_q)
    bias_t = jnp.where(sel, 0.0, NEG).T
    q_t = jnp.concatenate([q[:, r * HEAD_DIM:(r + 1) * HEAD_DIM].T for r in range(NSA_R)], axis=1)
    q_aug_t = jnp.concatenate([q_t, jnp.concatenate([bias_t] * NSA_R, axis=1)], axis=0).astype(BF16)
    tpos_l = s0 + (_iota((1, rows), 1) & (NSA_TQ - 1))
    m_ref[...] = jnp.full((1, rows), M_FLOOR, F32)
    l_ref[...] = jnp.zeros((1, rows), F32)
    acc_ref[...] = jnp.zeros((HEAD_DIM, rows), F32)

    def tile(kt, masked):
        s = _dot(ka_ref[0, pl.ds(pl.multiple_of(kt * NSA_TK, NSA_TK), NSA_TK), :], q_aug_t)
        if masked:
            s = jnp.where(kt * NSA_TK + _iota((NSA_TK, 1), 0) <= tpos_l, s, NEG)
        m_old = m_ref[...]
        m_new = jnp.maximum(m_old, jnp.max(s, axis=0, keepdims=True))
        p = jnp.exp2((s - m_new) * (SCALE * LOG2E))
        alpha = jnp.exp2((m_old - m_new) * (SCALE * LOG2E))
        l_ref[...] = alpha * l_ref[...] + jnp.sum(p, axis=0, keepdims=True)
        acc_ref[...] = alpha * acc_ref[...] + _dot(vst_ref[0, kt], p.astype(BF16))
        m_ref[...] = m_new

    def pair_body(j, carry):
        tile(2 * j, False)
        tile(2 * j + 1, False)
        return carry

    n_full = s0 >> 10
    lax.fori_loop(0, n_full >> 1, pair_body, 0)

    @pl.when((n_full & 1) == 1)
    def _():
        tile(n_full - 1, False)

    tile(n_full, True)
    o_s_t = acc_ref[...] * (1.0 / l_ref[...])

    gt = gt_ref[0]
    outs = []
    for r in range(NSA_R):
        rs = slice(r * NSA_TQ, (r + 1) * NSA_TQ)
        outs.append(gt[:, 3 * r:3 * r + 1] * o_c[rs] + gt[:, 3 * r + 1:3 * r + 2] * o_s_t[:, rs].T
                    + gt[:, 3 * r + 2:3 * r + 3] * o_w[rs])
    o_ref[...] = jnp.concatenate(outs, axis=1).astype(o_ref.dtype)


def _nsa_prompt(z, gates_t, k_cmp, v_cmp, k_aug, vs_t, kvb, t, kw_col, vw_col):
    assert t % NSA_TK == 0 and t >= WINDOW + NSA_TQ and NSA_TQ == HEAD_DIM
    nc = k_cmp.shape[0]
    rows = NSA_R * NSA_TQ
    return pl.pallas_call(
        _nsa_prompt_kernel, grid=(NSA_KV, t // NSA_TQ),
        in_specs=[pl.BlockSpec((NSA_TQ, NSA_R * HEAD_DIM), lambda g, i: (i, g)),
                  pl.BlockSpec((1, NSA_TQ, 3 * NSA_R), lambda g, i: (g, i, 0)),
                  pl.BlockSpec((nc, HEAD_DIM), lambda g, i: (0, g)),
                  pl.BlockSpec((nc, HEAD_DIM), lambda g, i: (0, g)),
                  pl.BlockSpec((1, t, 2 * HEAD_DIM), lambda g, i: (g, 0, 0)),
                  pl.BlockSpec((1,) + vs_t.shape[1:], lambda g, i: (g, 0, 0, 0)),
                  pl.BlockSpec((t, HEAD_DIM), lambda g, i: (0, kw_col + g)),
                  pl.BlockSpec((t, HEAD_DIM), lambda g, i: (0, vw_col + g))],
        out_specs=pl.BlockSpec((NSA_TQ, NSA_R * HEAD_DIM), lambda g, i: (i, g)),
        out_shape=jax.ShapeDtypeStruct((t, NSA_HEADS * HEAD_DIM), BF16),
        scratch_shapes=[pltpu.VMEM((1, rows), F32), pltpu.VMEM((1, rows), F32),
                        pltpu.VMEM((HEAD_DIM, rows), F32)],
        compiler_params=_cparams(("arbitrary", "arbitrary")),
        name="nsa_prompt",
    )(z, gates_t, k_cmp, v_cmp, k_aug, vs_t, kvb, kvb)


SB_T = 256
SB_R = SB_HEADS // NSA_KV


def _strict_upper(n):
    return (_iota((n, n), 0) > _iota((n, n), 1)).astype(BF16)


def _log2_sigmoids(z2):
    lb = jnp.minimum(z2, 0.0) - jnp.log2(1.0 + jnp.exp2(-jnp.abs(z2)))
    return lb, lb - z2


def _sb_tile_t(z_t, causal_t, carry, low):
    lb, lk = _log2_sigmoids(z_t)
    if causal_t is not None:
        lk = jnp.where(causal_t, lk, 0.0)
    after = _dot(low, lk.astype(BF16)) + carry
    a = jnp.exp2(lb + after)
    if causal_t is not None:
        a = jnp.where(causal_t, a, 0.0)
    return a, carry + jnp.sum(lk, axis=0, keepdims=True)


def _sb_prompt_kernel(qt_ref, k_ref, vt_ref, o_ref, carry_ref, acc_ref):
    i = pl.program_id(1)
    rows = SB_R * SB_T
    q_t = qt_ref[0, 0]
    low = (_iota((SB_T, SB_T), 0) < _iota((SB_T, SB_T), 1)).astype(BF16)

    causal_t = _iota((SB_T, 1), 0) < (_iota((1, rows), 1) & (SB_T - 1))
    z = _dot(k_ref[pl.ds(pl.multiple_of(i * SB_T, SB_T), SB_T), :], q_t)
    a, carry = _sb_tile_t(z, causal_t, jnp.zeros((1, rows), F32), low)
    carry_ref[...] = carry
    acc_ref[...] = _dot(vt_ref[0, i], a.astype(BF16))

    def left_tile(kj):
        zt = _dot(k_ref[pl.ds(pl.multiple_of(kj * SB_T, SB_T), SB_T), :], q_t)
        at, cn = _sb_tile_t(zt, None, carry_ref[...], low)
        carry_ref[...] = cn
        acc_ref[...] += _dot(vt_ref[0, kj], at.astype(BF16))

    def pair_body(j, c):
        left_tile(i - 1 - 2 * j)
        left_tile(i - 2 - 2 * j)
        return c

    lax.fori_loop(0, i >> 1, pair_body, 0)

    @pl.when((i & 1) == 1)
    def _():
        left_tile(0)
    acc = acc_ref[...]
    o_ref[...] = jnp.concatenate([acc[:, r * SB_T:(r + 1) * SB_T].T for r in range(SB_R)],
                                 axis=1).astype(o_ref.dtype)


def _sb_prompt(sq_t, sk, sv_t):
    t = sk.shape[0]
    assert t % SB_T == 0 and (SB_T & (SB_T - 1)) == 0
    rows = SB_R * SB_T
    return pl.pallas_call(
        _sb_prompt_kernel, grid=(NSA_KV, t // SB_T),
        in_specs=[pl.BlockSpec((1, 1, HEAD_DIM, rows), lambda g, i: (g, i, 0, 0)),
                  pl.BlockSpec((t, HEAD_DIM), lambda g, i: (0, g)),
                  pl.BlockSpec((1,) + sv_t.shape[1:], lambda g, i: (g, 0, 0, 0))],
        out_specs=pl.BlockSpec((SB_T, SB_R * HEAD_DIM), lambda g, i: (i, g)),
        out_shape=jax.ShapeDtypeStruct((t, SB_HEADS * HEAD_DIM), BF16),
        scratch_shapes=[pltpu.VMEM((1, rows), F32), pltpu.VMEM((HEAD_DIM, rows), F32)],
        compiler_params=_cparams(("arbitrary", "arbitrary")),
        name="sb_prompt",
    )(sq_t, sk, sv_t)


MEM_TQ = 512


def _mem_prompt_kernel(q_ref, k_ref, v_ref, o_ref):
    outs = []
    for h in range(MEM_HEADS):
        hs = slice(h * HEAD_DIM, (h + 1) * HEAD_DIM)
        s = _dot_nt(q_ref[:, hs].astype(BF16), k_ref[:, hs].astype(BF16)) * SCALE
        e = jnp.exp(s - jnp.max(s, axis=-1, keepdims=True))
        p = e * (1.0 / jnp.sum(e, axis=-1, keepdims=True))
        outs.append(_dot(p.astype(BF16), v_ref[:, hs].astype(BF16)))
    o_ref[...] = jnp.concatenate(outs, axis=1).astype(o_ref.dtype)


def _mem_prompt(z, mq_col, mk, mv, t):
    n_mem = mk.shape[0]
    w = MEM_HEADS * HEAD_DIM
    return pl.pallas_call(
        _mem_prompt_kernel, grid=(t // MEM_TQ,),
        in_specs=[pl.BlockSpec((MEM_TQ, w), lambda i: (i, mq_col)),
                  pl.BlockSpec((n_mem, w), lambda i: (0, 0)),
                  pl.BlockSpec((n_mem, w), lambda i: (0, 0))],
        out_specs=pl.BlockSpec((MEM_TQ, w), lambda i: (i, 0)),
        out_shape=jax.ShapeDtypeStruct((t, w), BF16),
        compiler_params=_cparams(("arbitrary",)),
        name="mem_prompt",
    )(z, mk, mv)


HROWS = 16


def _rows_of_group(n_rows, g, heads_per_group):
    row = _iota((n_rows, 1), 0)
    return (row >= g * heads_per_group) & (row < (g + 1) * heads_per_group)


def _group_lanes(full, heads_per_group):
    out = jnp.zeros((full.shape[0], HEAD_DIM), F32)
    for g in range(NSA_KV):
        out = out + jnp.where(_rows_of_group(full.shape[0], g, heads_per_group),
                              full[:, g * HEAD_DIM:(g + 1) * HEAD_DIM], 0.0)
    return out


def _token_rows(ref, n_tok, lead=()):
    return jnp.concatenate([ref[(*lead, pl.ds(g, n_tok, stride=NSA_KV), slice(None))] for g in range(NSA_KV)],
                           axis=1).astype(BF16)


def _page_specs(n_pages):
    return [pl.BlockSpec((PAGE * NSA_KV, HEAD_DIM), lambda b, pt, c=c: (pt[b * n_pages + c], 0))
            for c in range(n_pages)]


def _dec_sb_kernel(pt_ref, q_ref, *refs, n_pages):
    del pt_ref
    k_refs, v_refs, o_ref = refs[:n_pages], refs[n_pages:2 * n_pages], refs[2 * n_pages]
    q = q_ref[0]
    u = _strict_upper(PAGE)
    z = jnp.concatenate([_dot_nt(q, _token_rows(r, PAGE)) for r in k_refs], axis=1)
    lb, lk = _log2_sigmoids(z)
    lkb = lk.astype(BF16)
    page = lambda x, c: x[:, c * PAGE:(c + 1) * PAGE]
    carry = jnp.zeros((HROWS, 1), F32)
    acc = jnp.zeros((HROWS, KVW), F32)
    for c in reversed(range(n_pages)):
        after = _dot(page(lkb, c), u) + carry
        a = jnp.exp2(page(lb, c) + after)
        acc = acc + _dot(a.astype(BF16), _token_rows(v_refs[c], PAGE))
        carry = carry + jnp.sum(page(lk, c), axis=1, keepdims=True)
    o_ref[0] = _group_lanes(acc, SB_R).astype(o_ref.dtype)


def _dec_sb(page_ids, q_rows, cache_k, cache_v, n_pages):
    nb = q_rows.shape[0]
    grid_spec = pltpu.PrefetchScalarGridSpec(
        num_scalar_prefetch=1, grid=(nb,),
        in_specs=[pl.BlockSpec((1, HROWS, KVW), lambda b, pt: (b, 0, 0))] + 2 * _page_specs(n_pages),
        out_specs=pl.BlockSpec((1, HROWS, HEAD_DIM), lambda b, pt: (b, 0, 0)))
    return pl.pallas_call(
        functools.partial(_dec_sb_kernel, n_pages=n_pages), grid_spec=grid_spec,
        out_shape=jax.ShapeDtypeStruct((nb, HROWS, HEAD_DIM), BF16),
        compiler_params=_cparams(("arbitrary",)),
        name="dec_sb",
    )(page_ids, q_rows, *([cache_k] * n_pages), *([cache_v] * n_pages))


def _bf16_round(x):
    return x.astype(BF16).astype(F32)


def _pad_halves(x, half_rows):
    n = x.shape[0] // 2
    zeros = jnp.zeros((half_rows - n, x.shape[1]), x.dtype)
    return jnp.concatenate([x[:n], zeros, x[n:], zeros], axis=0)


def _dec_nsa_kernel(pt_ref, q_ref, gt_ref, kc_ref, vc_ref, wk_ref, wv_ref, nks_ref, nvs_ref, nkw_ref, nvw_ref,
                    *refs, past, n_pages):
    del pt_ref
    ks_refs, vs_refs, o_ref = refs[:n_pages], refs[n_pages:2 * n_pages], refs[2 * n_pages]
    q = q_ref[0]
    qf = q.astype(F32)

    nb = HEAD_DIM
    kc = _pad_halves(kc_ref[0].astype(BF16), nb)
    vc = _pad_halves(vc_ref[0].astype(BF16), nb)
    sc = _dot_nt(q, kc) * (SCALE * LOG2E)
    pc = _softmax_masked(sc, _cmp_block_end(2 * nb) <= past)
    o_c = _group_lanes(_dot(pc.astype(BF16), vc), NSA_R)
    imp = jnp.zeros((8, nb), F32)
    for g in range(NSA_KV):
        pcs = jnp.sum(pc[g * NSA_R:(g + 1) * NSA_R], axis=0, keepdims=True)
        imp = imp + jnp.where(_iota((8, 1), 0) == g, pcs[:, :nb] + pcs[:, nb:], 0.0)
    sel = _select_mask(imp, jnp.full((1, 1), past, jnp.int32)).astype(F32)
    pk = jnp.zeros((HROWS, nb), F32)
    for g in range(NSA_KV):
        pk = pk + jnp.where(_rows_of_group(HROWS, g, NSA_R), sel[g:g + 1, :], 0.0)

    s = jnp.concatenate([_dot_nt(q, _token_rows(r, PAGE)) for r in ks_refs], axis=1) * SCALE
    expand = (_iota((nb, past), 0) == (_iota((nb, past), 1) >> 6)).astype(BF16)
    s = jnp.where(_dot(pk.astype(BF16), expand) > 0.5, s, NEG)
    s_new = jnp.sum(qf * _bf16_round(nks_ref[0]), axis=1, keepdims=True) * SCALE
    nblk = past // SEL_BLOCK
    s_new = jnp.where(pk[:, nblk:nblk + 1] > 0.5, s_new, NEG)
    ms = jnp.maximum(jnp.maximum(jnp.max(s, axis=1, keepdims=True), s_new), M_FLOOR)
    e = jnp.exp(s - ms)
    e_new = jnp.exp(s_new - ms)
    ls = jnp.sum(e, axis=1, keepdims=True) + e_new
    acc = _bf16_round(e_new) * _bf16_round(nvs_ref[0])
    for c in range(n_pages):
        acc = acc + _dot(e[:, c * PAGE:(c + 1) * PAGE].astype(BF16), _token_rows(vs_refs[c], PAGE))
    o_s = _group_lanes(acc, NSA_R) * (1.0 / ls)

    wl = wk_ref.shape[1] // NSA_KV
    sw = _dot_nt(q, _token_rows(wk_ref, wl, (0,))) * SCALE
    wmask = past - wl + _iota((1, wl), 1) > past - WINDOW
    sw = jnp.where(wmask, sw, NEG)
    sw_new = jnp.sum(qf * _bf16_round(nkw_ref[0]), axis=1, keepdims=True) * SCALE
    mw = jnp.maximum(jnp.max(sw, axis=1, keepdims=True), sw_new)
    ew = jnp.where(wmask, jnp.exp(sw - mw), 0.0)
    ew_new = jnp.exp(sw_new - mw)
    lw = jnp.sum(ew, axis=1, keepdims=True) + ew_new
    accw = _dot(ew.astype(BF16), _token_rows(wv_ref, wl, (0,))) + _bf16_round(ew_new) * _bf16_round(nvw_ref[0])
    o_w = _group_lanes(accw, NSA_R) * (1.0 / lw)

    gt = gt_ref[0]
    o_ref[0] = (gt[:, 0:1] * o_c + gt[:, 1:2] * o_s + gt[:, 2:3] * o_w).astype(o_ref.dtype)


def _dec_nsa(page_ids, q_rows, gates, k_cmp, v_cmp, cache_k, cache_v, win_k, win_v, new_ks, new_vs, new_kw,
             new_vw, n_pages):
    nb = q_rows.shape[0]
    nc = k_cmp.shape[1]
    past = n_pages * PAGE
    assert nc <= 2 * HEAD_DIM and past // SEL_BLOCK < HEAD_DIM
    per_b = lambda shape: pl.BlockSpec((1,) + shape, lambda b, pt: (b, 0, 0))
    grid_spec = pltpu.PrefetchScalarGridSpec(
        num_scalar_prefetch=1, grid=(nb,),
        in_specs=[per_b((HROWS, KVW)), per_b((HROWS, 3)), per_b((nc, KVW)), per_b((nc, KVW)),
                  per_b(win_k.shape[1:]), per_b(win_v.shape[1:]),
                  per_b((1, KVW)), per_b((1, KVW)), per_b((1, KVW)), per_b((1, KVW))] + 2 * _page_specs(n_pages),
        out_specs=per_b((HROWS, HEAD_DIM)))
    return pl.pallas_call(
        functools.partial(_dec_nsa_kernel, past=past, n_pages=n_pages), grid_spec=grid_spec,
        out_shape=jax.ShapeDtypeStruct((nb, HROWS, HEAD_DIM), BF16),
        compiler_params=_cparams(("arbitrary",)),
        name="dec_nsa",
    )(page_ids, q_rows, gates, k_cmp, v_cmp, win_k, win_v, new_ks, new_vs, new_kw, new_vw,
      *([cache_k] * n_pages), *([cache_v] * n_pages))


def _dec_mem_kernel(q_ref, k_ref, v_ref, o_ref):
    n_mem = k_ref.shape[1] // NSA_KV
    s = _dot_nt(q_ref[0], _token_rows(k_ref, n_mem, (0,))) * SCALE
    e = jnp.exp(s - jnp.max(s, axis=-1, keepdims=True))
    p = e * (1.0 / jnp.sum(e, axis=-1, keepdims=True))
    o_ref[0] = _group_lanes(_dot(p.astype(BF16), _token_rows(v_ref, n_mem, (0,))), 1).astype(o_ref.dtype)


def _dec_mem(q_rows, mem_k, mem_v):
    nb = mem_k.shape[0]
    rows = q_rows.shape[1]
    per_b = lambda shape: pl.BlockSpec((1,) + shape, lambda b: (b, 0, 0))
    return pl.pallas_call(
        _dec_mem_kernel, grid=(nb,),
        in_specs=[per_b((rows, KVW)), per_b(mem_k.shape[1:]), per_b(mem_v.shape[1:])],
        out_specs=per_b((rows, HEAD_DIM)),
        out_shape=jax.ShapeDtypeStruct((nb, rows, HEAD_DIM), BF16),
        compiler_params=_cparams(("arbitrary",)),
        name="dec_mem",
    )(q_rows, mem_k, mem_v)


def _head_rows(q, n_heads, heads_per_group, rows):
    nb = q.shape[0]
    qh = q.reshape(nb, n_heads, 1, HEAD_DIM)
    onehot = (jnp.arange(n_heads)[:, None] // heads_per_group == jnp.arange(NSA_KV)[None, :])
    full = jnp.where(onehot[None, :, :, None], qh, 0.0).reshape(nb, n_heads, KVW)
    return jnp.pad(full, ((0, 0), (0, rows - n_heads), (0, 0))).astype(BF16)


def _even_odd(x, lanes_half):
    n = x.shape[-2]
    pad = [(0, 0)] * (x.ndim - 2) + [(0, lanes_half - n // 2), (0, 0)]
    return jnp.concatenate([jnp.pad(x[..., 0::2, :], pad), jnp.pad(x[..., 1::2, :], pad)], axis=-2)


def _rope_tables(pos):
    half = ROPE_DIMS // 2
    inv = jnp.power(ROPE_THETA, -jnp.arange(half, dtype=F32) * 2.0 / ROPE_DIMS)
    ang = pos.astype(F32)[:, None] * inv[None, :]
    cos, sin = jnp.cos(ang), jnp.sin(ang)
    n = pos.shape[0]
    rest = HEAD_DIM - ROPE_DIMS
    c = jnp.concatenate([cos, cos, jnp.ones((n, rest), F32)], axis=1)
    s_lo = jnp.concatenate([-sin, jnp.zeros((n, HEAD_DIM - half), F32)], axis=1)
    s_hi = jnp.concatenate([jnp.zeros((n, half), F32), sin, jnp.zeros((n, rest), F32)], axis=1)
    return c, s_lo, s_hi


C_Q, C_KC, C_VC, C_KS, C_VS, C_KW, C_VW = 0, 2048, 2560, 3072, 3584, 4096, 4608
C_SQ, C_SK, C_SV, C_MQ, C_BG, C_END = 5120, 6656, 7168, 7680, 8192, 20480
NG_COLS = NSA_HEADS * 3
NG_SRC = 5120


def kernel(x_prompt, mem_prompt, x_sample, cache_cmp_k, cache_cmp_v, cache_sel_k, cache_sel_v, cache_sb_k,
           cache_sb_v, state_win_k, state_win_v, cache_mem_k, cache_mem_v, page_table, norm1_w, w_in, g_nsa_q,
           g_cmp_k, g_sel_k, g_win_k, g_mem_q, g_mem_k, mem_norm_w, w_mem_kv, cmp_pos_k, cmp_k_w1, cmp_k_w2,
           cmp_pos_v, cmp_v_w1, cmp_v_w2, w_up_nsa, w_up_sb, w_up_mem, w_o, norm2_w, w_ff1, w_ff2):
    tp = x_prompt.shape[1]
    nb = x_sample.shape[0]
    n_pages = page_table.shape[1]
    past = n_pages * PAGE
    m = tp + nb
    tm = _row_tile(m, 832)
    assert x_prompt.shape[0] == 1 and x_sample.shape[1] == 1

    h = _rms_stack(x_prompt[0], x_sample[:, 0], norm1_w)
    w_in_t = w_in.T.astype(BF16)
    w_a = w_in_t[:NG_SRC]
    w_b = w_in_t[NG_SRC + NG_COLS:]
    w_ng = jnp.pad(w_in_t[NG_SRC:NG_SRC + NG_COLS].T, ((0, 0), (0, HEAD_DIM - NG_COLS)))
    pos = jnp.concatenate([jnp.arange(tp, dtype=jnp.int32), jnp.full((nb,), past, jnp.int32)])
    rope_c, rope_lo, rope_hi = _rope_tables(pos)
    modes = np.zeros((C_END // IN_TN,), np.int32)
    gidx = np.zeros((C_END // IN_TN,), np.int32)
    for c0, c1, md, gi in ((C_Q, C_KC, MODE_NORM_ROPE, 0), (C_KC, C_VC, MODE_NORM_ROPE, 1),
                           (C_KS, C_VS, MODE_NORM_ROPE, 2), (C_KW, C_VW, MODE_NORM_ROPE, 3),
                           (C_MQ, C_BG, MODE_NORM, 4)):
        modes[c0 // IN_TN:c1 // IN_TN] = md
        gidx[c0 // IN_TN:c1 // IN_TN] = gi
    gains = jnp.stack([g_nsa_q, g_cmp_k, g_sel_k, g_win_k, g_mem_q]).reshape(5, 1, HEAD_DIM)
    na = C_SQ // IN_TN
    z_a = _inproj(h, w_a, jnp.asarray(modes[:na]), jnp.asarray(gidx[:na]), gains, rope_c, rope_lo, rope_hi, tm)
    z_b = _inproj(h, w_b, jnp.asarray(modes[na:]), jnp.asarray(gidx[na:]), gains, rope_c, rope_lo, rope_hi, tm)
    ng = _mm(h, w_ng, tm=tm, tn=HEAD_DIM, tk=D_MODEL, out_dtype=F32, epi=_epi_sigmoid, name="head_gates")

    def zcols(c0, c1, r0, r1):
        arr, off = (z_a, 0) if c0 < C_SQ else (z_b, C_SQ)
        return arr[r0:r1, c0 - off:c1 - off]

    zp = lambda c0, c1: zcols(c0, c1, 0, tp)
    zs = lambda c0, c1: zcols(c0, c1, tp, m)
    kv_out = lambda a: a.reshape(a.shape[0], NSA_KV, HEAD_DIM)
    kv_cols = (C_KC, C_VC, C_KS, C_VS, C_SK, C_SV)
    p_kv = [kv_out(zp(c, c + KVW))[None] for c in kv_cols]
    s_kv = [kv_out(zs(c, c + KVW))[:, None] for c in kv_cols]
    keep = min(WINDOW, tp)
    p_win = [kv_out(zcols(c, c + KVW, tp - keep, tp))[None] for c in (C_KW, C_VW)]
    new_win = [kv_out(zs(c, c + KVW))[:, None] for c in (C_KW, C_VW)]
    lb = state_win_k.shape[1]
    keep_s = min(WINDOW, past + 1)
    s_win = [jnp.concatenate([st, nw], axis=1)[:, lb + 1 - keep_s:]
             for st, nw in ((state_win_k, new_win[0]), (state_win_v, new_win[1]))]

    w1k, w2k = cmp_k_w1.astype(BF16), cmp_k_w2.astype(BF16)
    w1v, w2v = cmp_v_w1.astype(BF16), cmp_v_w2.astype(BF16)
    ids_p = jnp.arange(tp // PAGE, dtype=jnp.int32)
    n_cmp = tp // CMP_BLOCK
    token_rows = lambda a: a.reshape(-1, HEAD_DIM)
    k_cmp_p = _even_odd(_compress(token_rows(p_kv[0]), ids_p, cmp_pos_k, w1k, w2k).reshape(n_cmp, KVW), HEAD_DIM)
    v_cmp_p = _even_odd(_compress(token_rows(p_kv[1]), ids_p, cmp_pos_v, w1v, w2v).reshape(n_cmp, KVW), HEAD_DIM)

    gates_all = ng[:, :NG_COLS].reshape(m, NSA_KV, 3 * NSA_R)
    gates_t = jnp.transpose(gates_all[:tp], (1, 0, 2))
    kvb = zp(C_KC, C_SQ).astype(BF16)
    blk_of_key = jnp.arange(tp, dtype=jnp.int32) // SEL_BLOCK
    onehot = (blk_of_key[:, None] == jnp.arange(HEAD_DIM, dtype=jnp.int32)[None, :]).astype(BF16)
    ks_b = jnp.transpose(zp(C_KS, C_VS).astype(BF16).reshape(tp, NSA_KV, HEAD_DIM), (1, 0, 2))
    k_aug = jnp.concatenate([ks_b, jnp.broadcast_to(onehot[None], ks_b.shape)], axis=2)
    vs_t = jnp.transpose(zp(C_VS, C_KW).astype(BF16).reshape(tp // NSA_TK, NSA_TK, NSA_KV, HEAD_DIM), (2, 0, 3, 1))
    cb = lambda c0: (c0 - C_KC) // HEAD_DIM
    o_nsa_p = _nsa_prompt(z_a, gates_t, k_cmp_p, v_cmp_p, k_aug, vs_t, kvb, tp, cb(C_KW), cb(C_VW))

    nq = tp // SB_T
    sq_t = jnp.transpose((zp(C_SQ, C_SK) * (SCALE * LOG2E)).astype(BF16).reshape(nq, SB_T, NSA_KV, SB_R, HEAD_DIM),
                         (2, 0, 4, 3, 1)).reshape(NSA_KV, nq, HEAD_DIM, SB_R * SB_T)
    sv_t = jnp.transpose(zp(C_SV, C_MQ).astype(BF16).reshape(nq, SB_T, NSA_KV, HEAD_DIM), (2, 0, 3, 1))
    o_sb_p = _sb_prompt(sq_t, zp(C_SK, C_SV).astype(BF16), sv_t)

    mem_h = _rms(mem_prompt[0], mem_norm_w, mem_prompt.shape[1])
    w_mkv = w_mem_kv.astype(BF16)
    n_mem = mem_prompt.shape[1]
    mw = MEM_HEADS * HEAD_DIM
    mk = _mm(mem_h, w_mkv, tm=n_mem, tn=mw, tk=D_MODEL, out_dtype=F32, epi=_epi_headnorm,
             extras=((g_mem_k.reshape(1, HEAD_DIM), 0),), ncols=mw, name="mem_k")
    mv = _mm(mem_h, w_mkv, tm=n_mem, tn=mw, tk=D_MODEL, out_dtype=F32, col0=1, name="mem_v")
    o_mem_p = _mem_prompt(z_b, (C_MQ - C_SQ) // mw, mk, mv, tp)

    pt = page_table.reshape(-1).astype(jnp.int32)
    n_cmp_s = past // CMP_BLOCK
    k_cmp_s = _even_odd(_compress(token_rows(cache_cmp_k), pt, cmp_pos_k, w1k, w2k), n_cmp_s // 2)
    v_cmp_s = _even_odd(_compress(token_rows(cache_cmp_v), pt, cmp_pos_v, w1v, w2v), n_cmp_s // 2)
    q_nsa_s = _head_rows(zs(C_Q, C_KC), NSA_HEADS, NSA_R, HROWS)
    gates_s = ng[tp:, :NG_COLS].reshape(nb, NSA_HEADS, 3)
    seq_rows = lambda a: a.reshape(a.shape[0], -1, HEAD_DIM)
    new_row = lambda c0: zs(c0, c0 + KVW).reshape(nb, 1, KVW)
    o_nsa_s = _dec_nsa(pt, q_nsa_s, gates_s, k_cmp_s, v_cmp_s, token_rows(cache_sel_k), token_rows(cache_sel_v),
                       seq_rows(state_win_k), seq_rows(state_win_v), new_row(C_KS), new_row(C_VS), new_row(C_KW),
                       new_row(C_VW), n_pages)
    q_sb_s = _head_rows(zs(C_SQ, C_SK) * (SCALE * LOG2E), SB_HEADS, SB_R, HROWS)
    o_sb_s = _dec_sb(pt, q_sb_s, token_rows(cache_sb_k), token_rows(cache_sb_v), n_pages)
    q_mem_s = _head_rows(zs(C_MQ, C_BG), MEM_HEADS, 1, 8)
    o_mem_s = _dec_mem(q_mem_s, seq_rows(cache_mem_k), seq_rows(cache_mem_v))

    o_nsa = jnp.concatenate([o_nsa_p, o_nsa_s.reshape(nb, NSA_HEADS * HEAD_DIM)], axis=0)
    o_sb = jnp.concatenate([o_sb_p, o_sb_s[:, :SB_HEADS].reshape(nb, SB_HEADS * HEAD_DIM)], axis=0)
    o_mem = jnp.concatenate([o_mem_p, o_mem_s[:, :MEM_HEADS].reshape(nb, mw)], axis=0)

    u = _merge_up(o_nsa, o_sb, o_mem, w_up_nsa.astype(BF16), w_up_sb.astype(BF16), w_up_mem.astype(BF16),
                  z_b, C_BG - C_SQ, tm, 512)
    x_all = jnp.concatenate([x_prompt[0], x_sample[:, 0]], axis=0)
    x1 = _mm(u, w_o.astype(BF16), tm=tm, tn=1024, tk=D_MODEL, out_dtype=F32, epi=_epi_residual,
             extras=((x_all, 0),), name="out_proj")
    h2 = _rms(x1, norm2_w, _row_tile(m, 512))
    a = _mm(h2, w_ff1.astype(BF16), tm=tm, tn=1024, tk=D_MODEL, out_dtype=BF16, epi=_epi_relu2, name="ff1")
    y = _mm(a, w_ff2.astype(BF16), tm=tm, tn=1024, tk=4096, out_dtype=F32, epi=_epi_residual,
            extras=((x1, 0),), name="ff2")

    y_prompt = y[:tp][None]
    y_sample = y[tp:][:, None]
    p_mem = [mk.reshape(1, n_mem, MEM_HEADS, HEAD_DIM), mv.reshape(1, n_mem, MEM_HEADS, HEAD_DIM)]
    return (y_prompt, y_sample, *p_kv, *p_win, *p_mem, *s_kv, *s_win)
```
